```python
import math
import jax
import jax.numpy as jnp
from jax import lax
import numpy as np

D_MODEL = 2048
BATCH = 2
SEQ = 4096
DEPTH = 4
DEC_BATCH = 8
DEC_SEQ = 8
PAST_LEN = 16384
PAGE_SIZE = 128

N_META = 16
N_MIXERS = 4
GROUP_WIDTH = D_MODEL // N_MIXERS
MIX_WIDTH = N_MIXERS * GROUP_WIDTH
A_HEADS = 4
A_DH = GROUP_WIDTH // (2 * A_HEADS)
A_QK = 2 * A_DH
A_DV = GROUP_WIDTH // A_HEADS
ROT_DIM = A_DH // 4
ROPE_THETA = 500000.0
Q_BLOCK = 128
B_HEADDIM = 64
B_HEADS = GROUP_WIDTH // B_HEADDIM
B_GROUPS = 2
B_HPG = B_HEADS // B_GROUPS
B_STATE = 128
B_CONV = 4
B_CONV_CH = GROUP_WIDTH + 2 * B_GROUPS * B_STATE
SSD_CHUNK = 128
C_HEAD = 64
C_HEADS = GROUP_WIDTH // C_HEAD
C_DECAY_LORA = 32
C_A_LORA = 32
C_V_LORA = 32
C_G_LORA = 96
C_LN_EPS = 64e-5
D_GROUP_CH = 16
D_GROUPS = GROUP_WIDTH // D_GROUP_CH
D_STATE = 64
FF_DIM = 5632
N_EXPERTS = 8
TOP_K = 2
MOE_BLOCK = 256
N_DENSE = (DEPTH + 1) // 2
N_MOE = DEPTH // 2
ALPHA = (2.0 * DEPTH) ** 0.25
BETA = (8.0 * DEPTH) ** -0.25
LN_EPS = 1e-5
RMS_EPS = 1e-5
NEG_INF = -1e30
N_IN = 9 * GROUP_WIDTH + B_CONV_CH + B_HEADS
SPLIT_IDX = (GROUP_WIDTH, 2 * GROUP_WIDTH, 3 * GROUP_WIDTH, 4 * GROUP_WIDTH,
             4 * GROUP_WIDTH + B_CONV_CH, 4 * GROUP_WIDTH + B_CONV_CH + B_HEADS,
             8 * GROUP_WIDTH + B_CONV_CH + B_HEADS)

kernel_name = 'hymba_diffattn_ssd_rwkv7_s5_decoder_step'


def layer_norm(x, g, b):
    xf = x.astype(jnp.float32)
    mu = jnp.mean(xf, axis=-1, keepdims=True)
    var = jnp.mean(jnp.square(xf - mu), axis=-1, keepdims=True)
    return ((xf - mu) * lax.rsqrt(var + LN_EPS) * g + b).astype(x.dtype)


def rope_partial(x, pos):
    half = ROT_DIM // 2
    inv = ROPE_THETA ** (-2.0 * jnp.arange(half, dtype=jnp.float32) / ROT_DIM)
    ang = pos.astype(jnp.float32)[:, None] * inv[None, :]
    cos = jnp.cos(ang)[None, :, None, None, :]
    sin = jnp.sin(ang)[None, :, None, None, :]
    xr = x[..., :ROT_DIM].astype(jnp.float32)
    x1, x2 = xr[..., :half], xr[..., half:]
    rot = jnp.concatenate([x1 * cos - x2 * sin, x2 * cos + x1 * sin], axis=-1)
    return jnp.concatenate([rot.astype(x.dtype), x[..., ROT_DIM:]], axis=-1)


def diff_softmax_blocks(q, k, v, q_pos, k_pos, lam):
    b, t = q.shape[:2]
    qb = min(Q_BLOCK, t)
    nb = -(-t // qb)
    pad = nb * qb - t
    qp = jnp.pad(q, ((0, 0), (0, pad), (0, 0), (0, 0), (0, 0)))
    qp = jnp.moveaxis(qp.reshape(b, nb, qb, A_HEADS, 2, A_DH), 1, 0)
    pp = jnp.pad(q_pos, (0, pad), mode='edge').reshape(nb, qb)
    scale = 1.0 / math.sqrt(A_DH)

    def one_block(args):
        qi, pi = args
        s = jnp.einsum('bqhcd,bkhcd->bhcqk', qi, k).astype(jnp.float32) * scale
        mask = k_pos[None, :] <= pi[:, None]
        s = jnp.where(mask, s, NEG_INF)
        p = jax.nn.softmax(s, axis=-1)
        a = p[:, :, 0] - lam * p[:, :, 1]
        return jnp.einsum('bhqk,bkhd->bqhd', a.astype(v.dtype), v)

    o = lax.map(one_block, (qp, pp))
    return jnp.moveaxis(o, 0, 1).reshape(b, nb * qb, A_HEADS, A_DV)[:, :t]


def diff_attention(q, k, v, pos, k_past, v_past, lam_vec, norm_g, lam_init):
    b, t, _ = q.shape
    q = rope_partial(q.reshape(b, t, A_HEADS, 2, A_DH), pos)
    k = rope_partial(k.reshape(b, t, A_HEADS, 2, A_DH), pos)
    v = v.reshape(b, t, A_HEADS, A_DV)
    k_rows = k.reshape(b, t, A_HEADS, A_QK)
    if k_past is None:
        k_all, v_all, k_pos = k, v, pos
    else:
        n_past = k_past.shape[1]
        k_all = jnp.concatenate([k_past.astype(k.dtype).reshape(b, n_past, A_HEADS, 2, A_DH), k], axis=1)
        v_all = jnp.concatenate([v_past.astype(v.dtype), v], axis=1)
        k_pos = jnp.concatenate([jnp.arange(n_past, dtype=jnp.int32), pos])
    lv = lam_vec.astype(jnp.float32)
    lam = jnp.exp(jnp.sum(lv[0] * lv[1])) - jnp.exp(jnp.sum(lv[2] * lv[3])) + lam_init
    o = diff_softmax_blocks(q, k_all, v_all, pos, k_pos, lam).astype(jnp.float32)
    o = o * lax.rsqrt(jnp.mean(jnp.square(o), axis=-1, keepdims=True) + RMS_EPS) * norm_g * (1.0 - lam_init)
    return o.reshape(b, t, GROUP_WIDTH).astype(q.dtype), k_rows, v


def ssd_scan(x, dt, a_neg, bm, cm, h0, front):
    f32 = jnp.float32
    b, t = x.shape[:2]
    q = SSD_CHUNK
    nc = -(-(front + t) // q)
    back = nc * q - front - t

    def chunks(arr):
        arr = jnp.pad(arr.astype(f32), ((0, 0), (front, back)) + ((0, 0),) * (arr.ndim - 2))
        return arr.reshape((b, nc, q) + arr.shape[2:])

    xc = chunks(x).reshape(b, nc, q, B_GROUPS, B_HPG, B_HEADDIM)
    dtc = chunks(dt).reshape(b, nc, q, B_GROUPS, B_HPG)
    bc = chunks(bm)
    cc = chunks(cm)
    acum = jnp.cumsum(dtc * a_neg.reshape(B_GROUPS, B_HPG), axis=2)
    causal = jnp.tril(jnp.ones((q, q), dtype=bool))[None, None, :, :, None, None]
    seg = acum[:, :, :, None] - acum[:, :, None, :]
    decay = jnp.exp(jnp.where(causal, seg, -jnp.inf))
    cb = jnp.einsum('bctgn,bcsgn->bctsg', cc, bc)
    w = cb[..., None] * decay * dtc[:, :, None]
    y_intra = jnp.einsum('bctsgr,bcsgrp->bctgrp', w, xc)
    to_end = jnp.exp(acum[:, :, -1:] - acum) * dtc
    chunk_state = jnp.einsum('bcsgn,bcsgr,bcsgrp->bcgrpn', bc, to_end, xc)
    chunk_decay = jnp.exp(acum[:, :, -1])

    def step(h, inp):
        dec, st = inp
        return dec[..., None, None] * h + st, h

    h_init = h0.astype(f32).reshape(b, B_GROUPS, B_HPG, B_HEADDIM, B_STATE)
    h_last, h_start = lax.scan(step, h_init, (jnp.moveaxis(chunk_decay, 1, 0), jnp.moveaxis(chunk_state, 1, 0)))
    h_start = jnp.moveaxis(h_start, 0, 1)
    y_inter = jnp.einsum('bctgn,bcgrpn,bctgr->bctgrp', cc, h_start, jnp.exp(acum))
    y = (y_intra + y_inter).reshape(b, nc * q, B_HEADS, B_HEADDIM)[:, front:front + t]
    return y, h_last.reshape(b, B_HEADS, B_HEADDIM, B_STATE)


def mamba2_mix(z, xbc, dt_raw, conv_buf, h0, conv_w, conv_b, dt_bias, a_log, d_skip, norm_g, front):
    f32 = jnp.float32
    b, t, _ = xbc.shape
    full = jnp.concatenate([conv_buf.astype(xbc.dtype), xbc], axis=1)
    conv = conv_b
    for j in range(B_CONV):
        conv = conv + full[:, j:j + t] * conv_w[j]
    new_buf = full[:, t:]
    xbc = jax.nn.silu(conv)
    xs = xbc[..., :GROUP_WIDTH].reshape(b, t, B_HEADS, B_HEADDIM)
    bm = xbc[..., GROUP_WIDTH:GROUP_WIDTH + B_GROUPS * B_STATE].reshape(b, t, B_GROUPS, B_STATE)
    cm = xbc[..., GROUP_WIDTH + B_GROUPS * B_STATE:].reshape(b, t, B_GROUPS, B_STATE)
    dt = jax.nn.softplus(dt_raw.astype(f32) + dt_bias.astype(f32))
    a_neg = -jnp.exp(a_log.astype(f32))
    y, h_last = ssd_scan(xs, dt, a_neg, bm, cm, h0, front)
    y = y + xs.astype(f32) * d_skip.astype(f32)[:, None]
    g = (y.reshape(b, t, GROUP_WIDTH) * jax.nn.silu(z.astype(f32))).reshape(b, t, B_GROUPS, GROUP_WIDTH // B_GROUPS)
    g = g * lax.rsqrt(jnp.mean(jnp.square(g), axis=-1, keepdims=True) + RMS_EPS)
    out = g.reshape(b, t, GROUP_WIDTH) * norm_g
    return out.astype(z.dtype), new_buf, h_last.astype(z.dtype)


def rwkv7_mix(rkvu, shift_prev, s0, v_first, vres, mu, w0, w1, w2, a0, a1, a2, g1, g2, k_k, k_a, r_k, ln_g, ln_b):
    f32 = jnp.float32
    b, t, _ = rkvu.shape
    prev = jnp.concatenate([shift_prev[:, None].astype(rkvu.dtype), rkvu[:, :-1]], axis=1)
    delta = prev - rkvu
    r_in, k_in, v_in, u_in = jnp.split(rkvu, 4, axis=-1)
    dr, dk, dv, du = jnp.split(delta, 4, axis=-1)
    r = r_in + dr * mu[0]
    k = k_in + dk * mu[1]
    v = v_in + dv * mu[2]
    xw = u_in + du * mu[3]
    xa = u_in + du * mu[4]
    xg = u_in + du * mu[5]
    w = -jax.nn.softplus(-(w0 + jnp.tanh(xw @ w1) @ w2).astype(f32)) - 0.5
    a = jax.nn.sigmoid((a0 + (xa @ a1) @ a2).astype(f32))
    g = jax.nn.sigmoid(xg @ g1) @ g2
    if vres is None:
        v_first = v
    else:
        v_mu, v0, v1, v2 = vres
        xv = u_in + du * v_mu
        v = v + (v_first - v) * jax.nn.sigmoid(v0 + (xv @ v1) @ v2)
    r, k, v, w, a = (z_.astype(f32).reshape(b, t, C_HEADS, C_HEAD) for z_ in (r, k, v, w, a))
    kk = k * k_k.astype(f32).reshape(C_HEADS, C_HEAD)
    kk = kk / jnp.maximum(jnp.linalg.norm(kk, axis=-1, keepdims=True), 1e-12)
    k = k * (1.0 + (a - 1.0) * k_a.astype(f32).reshape(C_HEADS, C_HEAD))
    decay = jnp.exp(-jnp.exp(w))

    def step(s, inp):
        r_t, k_t, v_t, kk_t, a_t, d_t = inp
        sa = jnp.einsum('bhij,bhj->bhi', s, -kk_t)
        s = s * d_t[:, :, None, :] + sa[..., None] * (kk_t * a_t)[:, :, None, :] + v_t[..., None] * k_t[:, :, None, :]
        return s, jnp.einsum('bhij,bhj->bhi', s, r_t)

    seq = tuple(jnp.moveaxis(z_, 1, 0) for z_ in (r, k, v, kk, a, decay))
    s_last, o = lax.scan(step, s0.astype(f32), seq)
    o = jnp.moveaxis(o, 0, 1)
    mu_o = jnp.mean(o, axis=-1, keepdims=True)
    var_o = jnp.mean(jnp.square(o - mu_o), axis=-1, keepdims=True)
    o = ((o - mu_o) * lax.rsqrt(var_o + C_LN_EPS)).reshape(b, t, GROUP_WIDTH) * ln_g + ln_b
    bonus = jnp.sum(r * k * r_k.astype(f32), axis=-1, keepdims=True) * v
    o = (o + bonus.reshape(b, t, GROUP_WIDTH)) * g
    return o.astype(rkvu.dtype), rkvu[:, -1], s_last.astype(rkvu.dtype), v_first


def s5_mix(u, h0_re, h0_im, a_re, a_im, log_dt, b_re, b_im, c_re, c_im, d_skip, glu_w, glu_b):
    f32 = jnp.float32
    b, t, _ = u.shape
    uf = u.astype(f32).reshape(b, t, D_GROUPS, D_GROUP_CH)
    lam = lax.complex(a_re.astype(f32), a_im.astype(f32))
    step = jnp.exp(log_dt.astype(f32))[:, None]
    abar = jnp.exp(lam * step)
    bbar = ((abar - 1.0) / lam)[..., None] * lax.complex(b_re.astype(f32), b_im.astype(f32))
    bu = jnp.einsum('gnc,btgc->btgn', bbar, uf.astype(jnp.complex64))
    h0 = lax.complex(h0_re.astype(f32), h0_im.astype(f32))
    bu = bu.at[:, 0].add(abar * h0)
    a_seq = jnp.broadcast_to(abar, bu.shape)

    def combine(e1, e2):
        a1, b1 = e1
        a2, b2 = e2
        return a1 * a2, a2 * b1 + b2

    _, h = lax.associative_scan(combine, (a_seq, bu), axis=1)
    cmat = lax.complex(c_re.astype(f32), c_im.astype(f32))
    y = jnp.real(jnp.einsum('gcn,btgn->btgc', cmat, h)) + d_skip.astype(f32).reshape(D_GROUPS, D_GROUP_CH) * uf
    y = jax.nn.gelu(y.reshape(b, t, GROUP_WIDTH))
    out = y * jax.nn.sigmoid(y @ glu_w + glu_b)
    h_last = h[:, -1]
    return out.astype(u.dtype), jnp.real(h_last).astype(u.dtype), jnp.imag(h_last).astype(u.dtype)


def swiglu(x, w1, w3, w2):
    return (jax.nn.silu(x @ w1) * (x @ w3)) @ w2


def moe_swiglu(x, router, w1, w3, w2, mi):
    f32 = jnp.float32
    b, t, d = x.shape
    n = b * t
    xf = x.reshape(n, d)
    logits = (xf @ router).astype(f32)
    top_v, top_i = lax.top_k(logits, TOP_K)
    gates = jax.nn.softmax(top_v, axis=-1)
    e_flat = top_i.reshape(-1)
    tok_flat = jnp.repeat(jnp.arange(n, dtype=jnp.int32), TOP_K)
    g_flat = gates.reshape(-1)
    per = max(1, -(-n * TOP_K // N_EXPERTS))
    m = min(MOE_BLOCK, max(8, 1 << (per - 1).bit_length()))
    n_blocks = -(-(n * TOP_K + N_EXPERTS * (m - 1)) // m)
    order = jnp.argsort(e_flat)
    e_sorted = e_flat[order]
    counts = jnp.bincount(e_flat, length=N_EXPERTS)
    padded = (counts + m - 1) // m * m
    pad_end = jnp.cumsum(padded)
    pad_start = pad_end - padded
    grp_start = jnp.cumsum(counts) - counts
    dest = pad_start[e_sorted] + jnp.arange(n * TOP_K) - grp_start[e_sorted]
    rows = n_blocks * m
    row_tok = jnp.full((rows,), n, jnp.int32).at[dest].set(tok_flat[order])
    row_gate = jnp.zeros((rows,), f32).at[dest].set(g_flat[order])
    block_exp = jnp.minimum(jnp.searchsorted(pad_end, jnp.arange(n_blocks) * m, side='right'), N_EXPERTS - 1)
    x_rows = jnp.concatenate([xf, jnp.zeros((1, d), xf.dtype)], axis=0)[row_tok].reshape(n_blocks, m, d)

    def expert_block(args):
        xb, e = args
        h = jax.nn.silu(xb @ w1[mi, e]) * (xb @ w3[mi, e])
        return h @ w2[mi, e]

    y_rows = lax.map(expert_block, (x_rows, block_exp)).reshape(rows, d)
    y = jnp.zeros((n + 1, d), f32).at[row_tok].add(y_rows.astype(f32) * row_gate[:, None])[:n]
    return y.astype(x.dtype).reshape(b, t, d)


def setup_inputs(seed: int = 0) -> dict:
    key = jax.random.key(seed)
    ks = list(jax.random.split(key, 80))
    f32 = jnp.float32

    def nrm(shape, scale=1.0):
        return scale * jax.random.normal(ks.pop(), shape, f32)

    def gain(shape):
        return 1.0 + 0.02 * jax.random.normal(ks.pop(), shape, f32)

    def unif(shape, lo, hi):
        return jax.random.uniform(ks.pop(), shape, f32, lo, hi)

    n_pages = PAST_LEN // PAGE_SIZE
    n_used = DEC_BATCH * n_pages
    n_pool = n_used + max(1, n_used // 4)
    gw = GROUP_WIDTH
    inp = {}
    inp['x_prompt'] = nrm((BATCH, SEQ, D_MODEL))
    inp['x_sample'] = nrm((DEC_BATCH, DEC_SEQ, D_MODEL))
    inp['cache_k'] = nrm((n_pool, DEPTH, PAGE_SIZE, A_HEADS, A_QK))
    inp['cache_v'] = nrm((n_pool, DEPTH, PAGE_SIZE, A_HEADS, A_DV))
    inp['state_ssm'] = nrm((DEC_BATCH, DEPTH, B_HEADS, B_HEADDIM, B_STATE), 0.1)
    inp['state_conv'] = nrm((DEC_BATCH, DEPTH, B_CONV - 1, B_CONV_CH))
    inp['state_wkv'] = nrm((DEC_BATCH, DEPTH, C_HEADS, C_HEAD, C_HEAD), 0.1)
    inp['state_shift'] = nrm((DEC_BATCH, DEPTH, 4 * gw))
    inp['state_s5_re'] = nrm((DEC_BATCH, DEPTH, D_GROUPS, D_STATE), 0.05)
    inp['state_s5_im'] = nrm((DEC_BATCH, DEPTH, D_GROUPS, D_STATE), 0.05)
    perm = jax.random.permutation(ks.pop(), n_pool)
    inp['page_table'] = perm[:n_used].reshape(DEC_BATCH, n_pages).astype(jnp.int32)
    inp['meta_tokens'] = nrm((N_META, D_MODEL))
    inp['ln_in_g'] = gain((D_MODEL,))
    inp['ln_in_b'] = nrm((D_MODEL,), 0.02)
    inp['w_in'] = nrm((DEPTH, D_MODEL, N_IN), D_MODEL ** -0.5)
    inp['w_out'] = nrm((DEPTH, MIX_WIDTH, D_MODEL), MIX_WIDTH ** -0.5 * BETA)
    inp['ln1_g'] = gain((DEPTH, D_MODEL))
    inp['ln1_b'] = nrm((DEPTH, D_MODEL), 0.02)
    inp['ln2_g'] = gain((DEPTH, D_MODEL))
    inp['ln2_b'] = nrm((DEPTH, D_MODEL), 0.02)
    inp['a_lambda'] = nrm((DEPTH, 4, A_DH), 0.1)
    inp['a_norm_g'] = gain((DEPTH, A_DV))
    inp['b_conv_w'] = nrm((DEPTH, B_CONV, B_CONV_CH), B_CONV ** -0.5)
    inp['b_conv_b'] = nrm((DEPTH, B_CONV_CH), 0.02)
    dt0 = jnp.exp(unif((DEPTH, B_HEADS), math.log(1e-3), math.log(1e-1)))
    inp['b_dt_bias'] = dt0 + jnp.log(-jnp.expm1(-dt0))
    inp['b_a_log'] = jnp.log(unif((DEPTH, B_HEADS), 1.0, 16.0))
    inp['b_d'] = gain((DEPTH, B_HEADS))
    inp['b_norm_g'] = gain((DEPTH, gw))
    inp['c_mu'] = unif((DEPTH, 6, gw), 0.0, 1.0)
    lin = jnp.arange(gw, dtype=f32) / (gw - 1)
    inp['c_w0'] = -6.5 + 5.0 * lin ** 0.85 + nrm((DEPTH, gw), 0.1)
    inp['c_w1'] = nrm((DEPTH, gw, C_DECAY_LORA), gw ** -0.5)
    inp['c_w2'] = nrm((DEPTH, C_DECAY_LORA, gw), 0.1 * C_DECAY_LORA ** -0.5)
    inp['c_a0'] = nrm((DEPTH, gw), 0.1)
    inp['c_a1'] = nrm((DEPTH, gw, C_A_LORA), gw ** -0.5)
    inp['c_a2'] = nrm((DEPTH, C_A_LORA, gw), 0.1 * C_A_LORA ** -0.5)
    inp['c_v_mu'] = unif((DEPTH - 1, gw), 0.0, 1.0)
    inp['c_v0'] = 1.0 + nrm((DEPTH - 1, gw), 0.1)
    inp['c_v1'] = nrm((DEPTH - 1, gw, C_V_LORA), gw ** -0.5)
    inp['c_v2'] = nrm((DEPTH - 1, C_V_LORA, gw), 0.1 * C_V_LORA ** -0.5)
    inp['c_g1'] = nrm((DEPTH, gw, C_G_LORA), gw ** -0.5)
    inp['c_g2'] = nrm((DEPTH, C_G_LORA, gw), C_G_LORA ** -0.5)
    inp['c_kk'] = 0.85 + nrm((DEPTH, gw), 0.02)
    inp['c_ka'] = gain((DEPTH, gw))
    inp['c_rk'] = nrm((DEPTH, C_HEADS, C_HEAD), 0.1)
    inp['c_ln_g'] = gain((DEPTH, gw))
    inp['c_ln_b'] = nrm((DEPTH, gw), 0.02)
    inp['d_a_re'] = -0.5 + nrm((DEPTH, D_GROUPS, D_STATE), 0.01)
    inp['d_a_im'] = math.pi * jnp.arange(D_STATE, dtype=f32) + nrm((DEPTH, D_GROUPS, D_STATE), 0.01)
    inp['d_log_dt'] = unif((DEPTH, D_GROUPS), math.log(1e-3), math.log(1e-1))
    inp['d_b_re'] = nrm((DEPTH, D_GROUPS, D_STATE, D_GROUP_CH), (2 * D_GROUP_CH) ** -0.5)
    inp['d_b_im'] = nrm((DEPTH, D_GROUPS, D_STATE, D_GROUP_CH), (2 * D_GROUP_CH) ** -0.5)
    inp['d_c_re'] = nrm((DEPTH, D_GROUPS, D_GROUP_CH, D_STATE), 0.5)
    inp['d_c_im'] = nrm((DEPTH, D_GROUPS, D_GROUP_CH, D_STATE), 0.5)
    inp['d_d'] = nrm((DEPTH, gw))
    inp['d_glu_w'] = nrm((DEPTH, gw, gw), gw ** -0.5)
    inp['d_glu_b'] = nrm((DEPTH, gw), 0.02)
    inp['f_w1'] = nrm((N_DENSE, D_MODEL, FF_DIM), D_MODEL ** -0.5)
    inp['f_w3'] = nrm((N_DENSE, D_MODEL, FF_DIM), D_MODEL ** -0.5)
    inp['f_w2'] = nrm((N_DENSE, FF_DIM, D_MODEL), FF_DIM ** -0.5 * BETA)
    inp['m_router'] = nrm((N_MOE, D_MODEL, N_EXPERTS), D_MODEL ** -0.5)
    inp['m_w1'] = nrm((N_MOE, N_EXPERTS, D_MODEL, FF_DIM), D_MODEL ** -0.5)
    inp['m_w3'] = nrm((N_MOE, N_EXPERTS, D_MODEL, FF_DIM), D_MODEL ** -0.5)
    inp['m_w2'] = nrm((N_MOE, N_EXPERTS, FF_DIM, D_MODEL), FF_DIM ** -0.5 * BETA)
    return inp


def reference(x_prompt, x_sample, cache_k, cache_v, state_ssm, state_conv, state_wkv, state_shift,
              state_s5_re, state_s5_im, page_table, meta_tokens, ln_in_g, ln_in_b, w_in, w_out,
              ln1_g, ln1_b, ln2_g, ln2_b, a_lambda, a_norm_g, b_conv_w, b_conv_b, b_dt_bias, b_a_log,
              b_d, b_norm_g, c_mu, c_w0, c_w1, c_w2, c_a0, c_a1, c_a2, c_v_mu, c_v0, c_v1, c_v2,
              c_g1, c_g2, c_kk, c_ka, c_rk, c_ln_g, c_ln_b, d_a_re, d_a_im, d_log_dt, d_b_re, d_b_im,
              d_c_re, d_c_im, d_d, d_glu_w, d_glu_b, f_w1, f_w3, f_w2, m_router, m_w1, m_w3, m_w2):

    def run_group(x_in, with_meta, start, paged, conv0, ssm0, shift0, wkv0, s5re0, s5im0):
        b = x_in.shape[0]
        if with_meta:
            meta = jnp.broadcast_to(meta_tokens.astype(x_in.dtype)[None], (b, N_META, D_MODEL))
            x = jnp.concatenate([meta, x_in], axis=1)
        else:
            x = x_in
        t = x.shape[1]
        pos = start + jnp.arange(t, dtype=jnp.int32)
        front = (start - N_META) % SSD_CHUNK
        x = layer_norm(x, ln_in_g, ln_in_b)
        ks, vs, ssms, convs, wkvs, shifts, s5res, s5ims = [], [], [], [], [], [], [], []
        v_first = None
        for l in range(DEPTH):
            proj = x @ w_in[l]
            q_a, k_a, v_a, z_b, xbc_b, dt_b, rkvu_c, u_d = jnp.split(proj, SPLIT_IDX, axis=-1)
            if paged:
                k_past = cache_k[page_table, l].reshape(b, -1, A_HEADS, A_QK)
                v_past = cache_v[page_table, l].reshape(b, -1, A_HEADS, A_DV)
            else:
                k_past = None
                v_past = None
            lam_init = 0.8 - 0.6 * math.exp(-0.3 * l)
            o_a, k_rows, v_rows = diff_attention(q_a, k_a, v_a, pos, k_past, v_past, a_lambda[l], a_norm_g[l], lam_init)
            o_b, conv_new, ssm_new = mamba2_mix(z_b, xbc_b, dt_b, conv0[:, l], ssm0[:, l], b_conv_w[l], b_conv_b[l],
                                                b_dt_bias[l], b_a_log[l], b_d[l], b_norm_g[l], front)
            vres = None if l == 0 else (c_v_mu[l - 1], c_v0[l - 1], c_v1[l - 1], c_v2[l - 1])
            o_c, shift_new, wkv_new, v_first = rwkv7_mix(rkvu_c, shift0[:, l], wkv0[:, l], v_first, vres, c_mu[l],
                                                        c_w0[l], c_w1[l], c_w2[l], c_a0[l], c_a1[l], c_a2[l],
                                                        c_g1[l], c_g2[l], c_kk[l], c_ka[l], c_rk[l], c_ln_g[l], c_ln_b[l])
            o_d, s5re_new, s5im_new = s5_mix(u_d, s5re0[:, l], s5im0[:, l], d_a_re[l], d_a_im[l], d_log_dt[l],
                                             d_b_re[l], d_b_im[l], d_c_re[l], d_c_im[l], d_d[l], d_glu_w[l], d_glu_b[l])
            mix = jnp.concatenate([o_a, o_b, o_c, o_d], axis=-1)
            x = layer_norm(ALPHA * x + mix @ w_out[l], ln1_g[l], ln1_b[l])
            if l % 2 == 0:
                f = swiglu(x, f_w1[l // 2], f_w3[l // 2], f_w2[l // 2])
            else:
                f = moe_swiglu(x, m_router[l // 2], m_w1, m_w3, m_w2, l // 2)
            x = layer_norm(ALPHA * x + f, ln2_g[l], ln2_b[l])
            ks.append(k_rows)
            vs.append(v_rows)
            ssms.append(ssm_new)
            convs.append(conv_new)
            wkvs.append(wkv_new)
            shifts.append(shift_new)
            s5res.append(s5re_new)
            s5ims.append(s5im_new)
        y = x[:, N_META:] if with_meta else x
        return (y, jnp.stack(ks, 1), jnp.stack(vs, 1), jnp.stack(ssms, 1), jnp.stack(convs, 1),
                jnp.stack(wkvs, 1), jnp.stack(shifts, 1), jnp.stack(s5res, 1), jnp.stack(s5ims, 1))

    bp = x_prompt.shape[0]
    dtp = x_prompt.dtype
    conv_z = jnp.zeros((bp, DEPTH, B_CONV - 1, B_CONV_CH), dtp)
    ssm_z = jnp.zeros((bp, DEPTH, B_HEADS, B_HEADDIM, B_STATE), dtp)
    shift_z = jnp.zeros((bp, DEPTH, 4 * GROUP_WIDTH), dtp)
    wkv_z = jnp.zeros((bp, DEPTH, C_HEADS, C_HEAD, C_HEAD), dtp)
    s5_z = jnp.zeros((bp, DEPTH, D_GROUPS, D_STATE), dtp)
    (y_p, k_p, v_p, ssm_p, conv_p, wkv_p, shift_p, s5re_p, s5im_p) = run_group(
        x_prompt, True, 0, False, conv_z, ssm_z, shift_z, wkv_z, s5_z, s5_z)
    past_len = page_table.shape[1] * PAGE_SIZE
    (y_s, k_s, v_s, ssm_s, conv_s, wkv_s, shift_s, s5re_s, s5im_s) = run_group(
        x_sample, False, past_len, True, state_conv, state_ssm, state_shift, state_wkv, state_s5_re, state_s5_im)
    return (y_p, y_s, k_p, v_p, k_s, v_s, ssm_p, ssm_s, conv_p, conv_s, wkv_p, wkv_s,
            shift_p, shift_s, s5re_p, s5im_p, s5re_s, s5im_s)
```

```python
import functools
import math

import jax
import jax.numpy as jnp
from jax import lax
from jax.experimental import pallas as pl
from jax.experimental.pallas import tpu as pltpu

F32 = jnp.float32
BF16 = jnp.bfloat16
HIGHEST = lax.Precision.HIGHEST

D_MODEL = 2048
DEPTH = 4
PAGE_SIZE = 128
N_META = 16
GROUP_WIDTH = 512
A_HEADS = 4
A_DH = 64
A_DV = 128
ROT_DIM = 16
ROPE_THETA = 500000.0
B_HEADS = 8
B_HEADDIM = 64
B_STATE = 128
B_CONV = 4
B_CONV_CH = 1024
C_HEADS = 8
C_HEAD = 64
C_LN_EPS = 64e-5
D_GROUPS = 32
D_GROUP_CH = 16
D_STATE = 64
FF_DIM = 5632
N_EXPERTS = 8
TOP_K = 2
ALPHA = (2.0 * DEPTH) ** 0.25
LN_EPS = 1e-5
RMS_EPS = 1e-5
NEG_BIG = -1e30

SEQ_PAD = 128
LANES = 128
V7X_VMEM_LIMIT = 56 * 1024 * 1024
MOE_BLOCK_ROWS = 512
N_PROJ = 11 * GROUP_WIDTH
COL_Q, COL_K, COL_V, COL_Z, COL_XBC, COL_RKVU, COL_S5 = 0, 512, 1024, 1536, 2048, 3072, 5120


def _tile(n, target, mult):
    best = None
    for d in range(mult, min(n, target) + 1, mult):
        if n % d == 0:
            best = d
    return n if best is None else best


def _params(sem, vmem=None):
    return pltpu.CompilerParams(dimension_semantics=sem, vmem_limit_bytes=vmem)


def _sigmoid(x):
    return 1.0 / (1.0 + jnp.exp(-x))


def _softplus(x):
    return jnp.maximum(x, 0.0) + jnp.log(1.0 + jnp.exp(-jnp.abs(x)))


def _ln_rows(x, g, b):
    mu = jnp.mean(x, axis=-1, keepdims=True)
    xc = x - mu
    var = jnp.mean(xc * xc, axis=-1, keepdims=True)
    return xc * lax.rsqrt(var + LN_EPS) * g + b


def _dot(a, b):
    return jnp.dot(a, b, preferred_element_type=F32)


def _dot_nt(a, b):
    return lax.dot_general(a, b, (((1,), (1,)), ((), ())), preferred_element_type=F32)


def _seg_dot(x, p):
    hi = x.astype(BF16)
    lo = (x - hi.astype(F32)).astype(BF16)
    return _dot(hi, p) + _dot(lo, p)


def _ln_in_body(x_ref, g_ref, b_ref, o_ref, ob_ref):
    y = _ln_rows(x_ref[...], g_ref[...], b_ref[...])
    o_ref[...] = y
    ob_ref[...] = y.astype(BF16)


def ln_in(x, g, b):
    m, d = x.shape
    tm = _tile(m, 512, 16)
    return pl.pallas_call(
        _ln_in_body, name="ln_in", grid=(m // tm,),
        in_specs=[pl.BlockSpec((tm, d), lambda i: (i, 0)), pl.BlockSpec((1, d), lambda i: (0, 0)),
                  pl.BlockSpec((1, d), lambda i: (0, 0))],
        out_specs=[pl.BlockSpec((tm, d), lambda i: (i, 0)), pl.BlockSpec((tm, d), lambda i: (i, 0))],
        out_shape=[jax.ShapeDtypeStruct((m, d), F32), jax.ShapeDtypeStruct((m, d), BF16)],
        compiler_params=_params(("parallel",)),
    )(x, g.reshape(1, d), b.reshape(1, d))


def _mm_body(x_ref, w_ref, o_ref):
    o_ref[...] = _dot(x_ref[...], w_ref[...].astype(BF16))


def matmul_layer(xb, w, layer, tn, name):
    m, k = xb.shape
    n = w.shape[2]
    tm = _tile(m, 1280, 16)
    return pl.pallas_call(
        _mm_body, name=name, grid=(m // tm, n // tn),
        in_specs=[pl.BlockSpec((tm, k), lambda i, j: (i, 0)),
                  pl.BlockSpec((None, k, tn), lambda i, j: (layer, 0, j))],
        out_specs=pl.BlockSpec((tm, tn), lambda i, j: (i, j)),
        out_shape=jax.ShapeDtypeStruct((m, n), F32),
        compiler_params=_params(("parallel", "arbitrary"), V7X_VMEM_LIMIT),
    )(xb, w)


def _router_body(x_ref, w_ref, o_ref):
    o_ref[...] = jnp.dot(x_ref[...], w_ref[...], precision=HIGHEST, preferred_element_type=F32)


def router_logits(x, w, layer):
    m, k = x.shape
    n = w.shape[2]
    tm = _tile(m, 512, 8)
    return pl.pallas_call(
        _router_body, name="router", grid=(m // tm,),
        in_specs=[pl.BlockSpec((tm, k), lambda i: (i, 0)), pl.BlockSpec((None, k, n), lambda i: (layer, 0, 0))],
        out_specs=pl.BlockSpec((tm, n), lambda i: (i, 0)),
        out_shape=jax.ShapeDtypeStruct((m, n), F32),
        compiler_params=_params(("parallel",), V7X_VMEM_LIMIT),
    )(x, w)


def _glu_body(x_ref, w1_ref, w3_ref, o_ref):
    x = x_ref[...]
    a = _dot(x, w1_ref[...].astype(BF16))
    b = _dot(x, w3_ref[...].astype(BF16))
    o_ref[...] = (a * _sigmoid(a) * b).astype(BF16)


def swiglu_up(xb, w1, w3, layer):
    m, k = xb.shape
    n = w1.shape[2]
    tm = _tile(m, 1280, 16)
    tn = 512
    wspec = pl.BlockSpec((None, k, tn), lambda i, j: (layer, 0, j))
    return pl.pallas_call(
        _glu_body, name="swiglu_up", grid=(m // tm, n // tn),
        in_specs=[pl.BlockSpec((tm, k), lambda i, j: (i, 0)), wspec, wspec],
        out_specs=pl.BlockSpec((tm, tn), lambda i, j: (i, j)),
        out_shape=jax.ShapeDtypeStruct((m, n), BF16),
        compiler_params=_params(("parallel", "arbitrary"), V7X_VMEM_LIMIT),
    )(xb, w1, w3)


def _mm_ln_body(h_ref, w_ref, r_ref, g_ref, b_ref, o_ref, ob_ref, acc_ref, *, nk):
    k = pl.program_id(1)

    @pl.when(k == 0)
    def _():
        acc_ref[...] = jnp.zeros_like(acc_ref)

    acc_ref[...] += _dot(h_ref[...], w_ref[...].astype(BF16))

    @pl.when(k == nk - 1)
    def _():
        y = _ln_rows(ALPHA * r_ref[...] + acc_ref[...], g_ref[...], b_ref[...])
        o_ref[...] = y
        ob_ref[...] = y.astype(BF16)


def matmul_resid_ln(hb, w, layer, resid, g, b, name):
    m, k = hb.shape
    n = w.shape[2]
    tm = _tile(m, 640, 16)
    tk = 512
    nk = k // tk
    row = pl.BlockSpec((tm, n), lambda i, kk: (i, 0))
    vec = pl.BlockSpec((1, n), lambda i, kk: (0, 0))
    return pl.pallas_call(
        functools.partial(_mm_ln_body, nk=nk), name=name, grid=(m // tm, nk),
        in_specs=[pl.BlockSpec((tm, tk), lambda i, kk: (i, kk)),
                  pl.BlockSpec((None, tk, n), lambda i, kk: (layer, kk, 0)), row, vec, vec],
        out_specs=[row, row],
        out_shape=[jax.ShapeDtypeStruct((m, n), F32), jax.ShapeDtypeStruct((m, n), BF16)],
        scratch_shapes=[pltpu.VMEM((tm, n), F32)],
        compiler_params=_params(("parallel", "arbitrary"), V7X_VMEM_LIMIT),
    )(hb, w, resid, g.reshape(1, n), b.reshape(1, n))


def _gather_body(idx_ref, src_ref, o_ref, buf, sem, *, br):
    base = pl.program_id(0) * br

    def row_copy(r, src_row):
        return pltpu.make_async_copy(src_ref.at[pl.ds(src_row, 1)], buf.at[pl.ds(r, 1)], sem)

    def start(r, c):
        row_copy(r, idx_ref[base + r]).start()
        return c

    def wait(r, c):
        row_copy(r, 0).wait()
        return c

    lax.fori_loop(0, br, start, 0)
    lax.fori_loop(0, br, wait, 0)
    o_ref[...] = buf[...].astype(BF16)


def gather_rows_bf16(src, idx):
    r = idx.shape[0]
    d = src.shape[1]
    br = _tile(r, 256, 16)
    return pl.pallas_call(
        functools.partial(_gather_body, br=br), name="moe_gather",
        grid_spec=pltpu.PrefetchScalarGridSpec(
            num_scalar_prefetch=1, grid=(r // br,),
            in_specs=[pl.BlockSpec(memory_space=pl.ANY)],
            out_specs=pl.BlockSpec((br, d), lambda i, idx_ref: (i, 0)),
            scratch_shapes=[pltpu.VMEM((br, d), F32), pltpu.SemaphoreType.DMA(())]),
        out_shape=jax.ShapeDtypeStruct((r, d), BF16),
        compiler_params=_params(("arbitrary",)),
    )(idx, src)


def _moe_up_body(be_ref, bv_ref, x_ref, w1_ref, w3_ref, o_ref):
    @pl.when(bv_ref[pl.program_id(1)] > 0)
    def _():
        _glu_body(x_ref, w1_ref, w3_ref, o_ref)

    @pl.when(bv_ref[pl.program_id(1)] == 0)
    def _():
        o_ref[...] = jnp.zeros_like(o_ref)


def moe_up(xg, w1, w3, mi, block_exp, block_valid, bm):
    r, k = xg.shape
    n = w1.shape[3]
    tn = 512
    wspec = pl.BlockSpec((None, None, k, tn), lambda j, i, be, bv: (mi, be[i], 0, j))
    return pl.pallas_call(
        _moe_up_body, name="moe_up",
        grid_spec=pltpu.PrefetchScalarGridSpec(
            num_scalar_prefetch=2, grid=(n // tn, r // bm),
            in_specs=[pl.BlockSpec((bm, k), lambda j, i, be, bv: (i, 0)), wspec, wspec],
            out_specs=pl.BlockSpec((bm, tn), lambda j, i, be, bv: (i, j))),
        out_shape=jax.ShapeDtypeStruct((r, n), BF16),
        compiler_params=_params(("arbitrary", "arbitrary"), V7X_VMEM_LIMIT),
    )(block_exp, block_valid, xg, w1, w3)


def _moe_down_body(be_ref, bv_ref, h_ref, w_ref, o_ref):
    @pl.when(bv_ref[pl.program_id(1)] > 0)
    def _():
        _mm_body(h_ref, w_ref, o_ref)

    @pl.when(bv_ref[pl.program_id(1)] == 0)
    def _():
        o_ref[...] = jnp.zeros_like(o_ref)


def moe_down(hg, w2, mi, block_exp, block_valid, bm):
    r, k = hg.shape
    n = w2.shape[3]
    tn = 256
    return pl.pallas_call(
        _moe_down_body, name="moe_down",
        grid_spec=pltpu.PrefetchScalarGridSpec(
            num_scalar_prefetch=2, grid=(n // tn, r // bm),
            in_specs=[pl.BlockSpec((bm, k), lambda j, i, be, bv: (i, 0)),
                      pl.BlockSpec((None, None, k, tn), lambda j, i, be, bv: (mi, be[i], 0, j))],
            out_specs=pl.BlockSpec((bm, tn), lambda j, i, be, bv: (i, j))),
        out_shape=jax.ShapeDtypeStruct((r, n), F32),
        compiler_params=_params(("arbitrary", "arbitrary"), V7X_VMEM_LIMIT),
    )(block_exp, block_valid, hg, w2)


def _combine_body(d0_ref, d1_ref, y_ref, g0_ref, g1_ref, r_ref, g_ref, b_ref, o_ref, ob_ref, buf0, buf1, sems, *, tm):
    base = pl.program_id(0) * tm

    def row_copy(r, src_row, buf, k):
        return pltpu.make_async_copy(y_ref.at[pl.ds(src_row, 1)], buf.at[pl.ds(r, 1)], sems.at[k])

    def start(r, c):
        row_copy(r, d0_ref[base + r], buf0, 0).start()
        row_copy(r, d1_ref[base + r], buf1, 1).start()
        return c

    def wait(r, c):
        row_copy(r, 0, buf0, 0).wait()
        row_copy(r, 0, buf1, 1).wait()
        return c

    lax.fori_loop(0, tm, start, 0)
    lax.fori_loop(0, tm, wait, 0)
    f = g0_ref[...] * buf0[...] + g1_ref[...] * buf1[...]
    y = _ln_rows(ALPHA * r_ref[...] + f, g_ref[...], b_ref[...])
    o_ref[...] = y
    ob_ref[...] = y.astype(BF16)


def moe_combine_ln(y_rows, d0, d1, g0, g1, resid, g, b):
    m, n = resid.shape
    tm = _tile(m, 256, 16)
    row = pl.BlockSpec((tm, n), lambda i, a, c: (i, 0))
    col = pl.BlockSpec((tm, 1), lambda i, a, c: (i, 0))
    vec = pl.BlockSpec((1, n), lambda i, a, c: (0, 0))
    return pl.pallas_call(
        functools.partial(_combine_body, tm=tm), name="moe_combine",
        grid_spec=pltpu.PrefetchScalarGridSpec(
            num_scalar_prefetch=2, grid=(m // tm,),
            in_specs=[pl.BlockSpec(memory_space=pl.ANY), col, col, row, vec, vec],
            out_specs=[row, row],
            scratch_shapes=[pltpu.VMEM((tm, n), F32), pltpu.VMEM((tm, n), F32), pltpu.SemaphoreType.DMA((2,))]),
        out_shape=[jax.ShapeDtypeStruct((m, n), F32), jax.ShapeDtypeStruct((m, n), BF16)],
        compiler_params=_params(("arbitrary",), V7X_VMEM_LIMIT),
    )(d0, d1, y_rows, g0, g1, resid, g.reshape(1, n), b.reshape(1, n))


def moe_layer(x, xb, router_w, w1, w3, w2, mi, ln_g, ln_b):
    m = x.shape[0]
    bm = MOE_BLOCK_ROWS
    logits = router_logits(x, router_w, mi)[:, :N_EXPERTS]
    top_v, top_i = lax.top_k(logits, TOP_K)
    gates = jax.nn.softmax(top_v, axis=-1)
    e_flat = top_i.reshape(-1)
    n_assign = m * TOP_K
    onehot = (e_flat[:, None] == jnp.arange(N_EXPERTS, dtype=e_flat.dtype)[None, :]).astype(jnp.int32)
    rank = jnp.take_along_axis(jnp.cumsum(onehot, axis=0) - onehot, e_flat[:, None], axis=1)[:, 0]
    counts = jnp.sum(onehot, axis=0)
    padded = (counts + bm - 1) // bm * bm
    pad_end = jnp.cumsum(padded)
    pad_start = pad_end - padded
    dest = (pad_start[e_flat] + rank).astype(jnp.int32)
    n_blocks = -(-(n_assign + N_EXPERTS * (bm - 1)) // bm)
    rows = n_blocks * bm
    row_tok = jnp.zeros((rows,), jnp.int32).at[dest].set(jnp.arange(n_assign, dtype=jnp.int32) // TOP_K)
    blk_start = jnp.arange(n_blocks, dtype=jnp.int32) * bm
    block_exp = jnp.minimum(jnp.searchsorted(pad_end, blk_start, side='right'), N_EXPERTS - 1).astype(jnp.int32)
    block_valid = (blk_start < pad_end[-1]).astype(jnp.int32)
    xg = gather_rows_bf16(x, row_tok)
    hg = moe_up(xg, w1, w3, mi, block_exp, block_valid, bm)
    yg = moe_down(hg, w2, mi, block_exp, block_valid, bm)
    dest2 = dest.reshape(m, TOP_K)
    return moe_combine_ln(yg, dest2[:, 0], dest2[:, 1], gates[:, 0:1], gates[:, 1:2], x, ln_g, ln_b)


def _rope_tables(pos):
    half = ROT_DIM // 2
    inv = ROPE_THETA ** (-2.0 * jnp.arange(half, dtype=F32) / ROT_DIM)
    ang = pos.astype(F32)[:, None] * inv[None, :]
    cos, sin = jnp.cos(ang), jnp.sin(ang)
    t = pos.shape[0]
    z8 = jnp.zeros((t, half), F32)
    z48 = jnp.zeros((t, A_DH - ROT_DIM), F32)
    c64 = jnp.concatenate([cos, cos, jnp.ones((t, A_DH - ROT_DIM), F32)], axis=1)
    s_up = jnp.concatenate([z8, sin, z48], axis=1)
    s_dn = jnp.concatenate([-sin, z8, z48], axis=1)
    return tuple(jnp.tile(a, (1, LANES // A_DH)) for a in (c64, s_up, s_dn))


def _rope_body(q_ref, k_ref, c_ref, su_ref, sd_ref, qo_ref, ko_ref):
    c, su, sd = c_ref[...], su_ref[...], sd_ref[...]
    half = ROT_DIM // 2
    for g in range(GROUP_WIDTH // LANES):
        sl = slice(g * LANES, (g + 1) * LANES)
        for src, dst, scale in ((q_ref, qo_ref, 1.0 / math.sqrt(A_DH)), (k_ref, ko_ref, 1.0)):
            x = src[:, sl]
            y = x * c + pltpu.roll(x, half, 1) * su + pltpu.roll(x, LANES - half, 1) * sd
            dst[:, sl] = (y * scale).astype(dst.dtype)


def rope_qk(proj, tables, b, t):
    rows = b * t
    tq = _tile(t, 1024, SEQ_PAD)
    nq = t // tq
    gw = GROUP_WIDTH
    tab = pl.BlockSpec((tq, LANES), lambda i: (i % nq, 0))
    return pl.pallas_call(
        _rope_body, name="rope", grid=(rows // tq,),
        in_specs=[pl.BlockSpec((tq, gw), lambda i: (i, COL_Q // gw)),
                  pl.BlockSpec((tq, gw), lambda i: (i, COL_K // gw)), tab, tab, tab],
        out_specs=[pl.BlockSpec((tq, gw), lambda i: (i, 0)), pl.BlockSpec((tq, gw), lambda i: (i, 0))],
        out_shape=[jax.ShapeDtypeStruct((rows, gw), BF16), jax.ShapeDtypeStruct((rows, gw), F32)],
        compiler_params=_params(("parallel",)),
    )(proj, proj, *tables)


def _stack_maps(q):
    lane = lax.broadcasted_iota(jnp.int32, q.shape, 1)
    zero = jnp.zeros_like(q)
    return jnp.concatenate([jnp.where(lane < A_DH, q, zero), jnp.where(lane >= A_DH, q, zero)], axis=0)


def _online_softmax_step(s, v, m_ref, l_ref, acc_ref):
    m_prev = m_ref[...]
    m_new = jnp.maximum(m_prev, jnp.max(s, axis=-1, keepdims=True))
    a = jnp.exp(m_prev - m_new)
    p = jnp.exp(s - m_new)
    l_ref[...] = a * l_ref[...] + jnp.sum(p, axis=-1, keepdims=True)
    acc_ref[...] = a * acc_ref[...] + _dot(p.astype(BF16), v)
    m_ref[...] = m_new


def _diff_finalize(lam_ref, g_ref, l, acc, tq, lam_init):
    lv = lam_ref[...]
    lam = (jnp.exp(jnp.sum(lv[0:1] * lv[1:2], axis=-1, keepdims=True))
           - jnp.exp(jnp.sum(lv[2:3] * lv[3:4], axis=-1, keepdims=True)) + lam_init)
    o = acc[0:tq] / l[0:tq] - lam * (acc[tq:2 * tq] / l[tq:2 * tq])
    o = o * lax.rsqrt(jnp.mean(o * o, axis=-1, keepdims=True) + RMS_EPS) * g_ref[...] * (1.0 - lam_init)
    return o


def _flash_body(lam_ref, g_ref, q_ref, k_ref, v_ref, o_ref, qq_ref, m_ref, l_ref, acc_ref, *, tq, nk, lam_init):
    qi = pl.program_id(2)
    ki = pl.program_id(3)

    @pl.when(ki == 0)
    def _():
        qq_ref[...] = _stack_maps(q_ref[...])
        m_ref[...] = jnp.full_like(m_ref, NEG_BIG)
        l_ref[...] = jnp.zeros_like(l_ref)
        acc_ref[...] = jnp.zeros_like(acc_ref)

    @pl.when(ki <= qi)
    def _():
        s = _dot_nt(qq_ref[...], k_ref[...].astype(BF16))
        r = lax.broadcasted_iota(jnp.int32, s.shape, 0)
        r = jnp.where(r >= tq, r - tq, r)
        c = lax.broadcasted_iota(jnp.int32, s.shape, 1)
        s = jnp.where(ki * tq + c <= qi * tq + r, s, NEG_BIG)
        _online_softmax_step(s, v_ref[...].astype(BF16), m_ref, l_ref, acc_ref)

    @pl.when(ki == nk - 1)
    def _():
        o_ref[...] = _diff_finalize(lam_ref, g_ref, l_ref[...], acc_ref[...], tq, lam_init).astype(o_ref.dtype)


def flash_diff_attention(qs, kr, proj, lam_vec, norm_g, b, t, lam_init):
    tq = _tile(t, 512, SEQ_PAD)
    nq = t // tq
    vcol = COL_V // A_DV
    kv_map = lambda col0: (lambda bi, h, qi, ki: (bi * nq + jnp.minimum(ki, qi), col0 + h))
    return pl.pallas_call(
        functools.partial(_flash_body, tq=tq, nk=nq, lam_init=lam_init), name="flash_diff_attn",
        grid=(b, A_HEADS, nq, nq),
        in_specs=[pl.BlockSpec((4, A_DH), lambda bi, h, qi, ki: (0, 0)),
                  pl.BlockSpec((1, A_DV), lambda bi, h, qi, ki: (0, 0)),
                  pl.BlockSpec((tq, A_DV), lambda bi, h, qi, ki: (bi * nq + qi, h)),
                  pl.BlockSpec((tq, A_DV), kv_map(0)),
                  pl.BlockSpec((tq, A_DV), kv_map(vcol))],
        out_specs=pl.BlockSpec((tq, A_DV), lambda bi, h, qi, ki: (bi * nq + qi, h)),
        out_shape=jax.ShapeDtypeStruct((b * t, GROUP_WIDTH), BF16),
        scratch_shapes=[pltpu.VMEM((2 * tq, A_DV), BF16), pltpu.VMEM((2 * tq, 1), F32),
                        pltpu.VMEM((2 * tq, 1), F32), pltpu.VMEM((2 * tq, A_DV), F32)],
        compiler_params=_params(("parallel", "parallel", "parallel", "arbitrary")),
    )(lam_vec, norm_g.reshape(1, A_DV), qs, kr, proj)


SAMPLE_Q_ROWS = 16


def _paged_body(pt_ref, lam_ref, g_ref, q_ref, kc_ref, vc_ref, kn_ref, vn_ref, o_ref, qq_ref, m_ref, l_ref, acc_ref,
                *, n_pages, lam_init):
    p = pl.program_id(1)
    tq = SAMPLE_Q_ROWS

    @pl.when(p == 0)
    def _():
        for h in range(A_HEADS):
            qq_ref[h] = _stack_maps(q_ref[:, h * A_DV:(h + 1) * A_DV])
        m_ref[...] = jnp.full_like(m_ref, NEG_BIG)
        l_ref[...] = jnp.zeros_like(l_ref)
        acc_ref[...] = jnp.zeros_like(acc_ref)

    def heads(k_ref, v_ref, new_tokens):
        for h in range(A_HEADS):
            sl = slice(h * A_DV, (h + 1) * A_DV)
            s = _dot_nt(qq_ref[h], k_ref[:, sl].astype(BF16))
            if new_tokens:
                r = lax.broadcasted_iota(jnp.int32, s.shape, 0)
                r = jnp.where(r >= tq, r - tq, r)
                c = lax.broadcasted_iota(jnp.int32, s.shape, 1)
                s = jnp.where(c <= r, s, NEG_BIG)
            _online_softmax_step(s, v_ref[:, sl].astype(BF16), m_ref.at[h], l_ref.at[h], acc_ref.at[h])

    @pl.when(p < n_pages)
    def _():
        heads(kc_ref, vc_ref, False)

    @pl.when(p == n_pages)
    def _():
        heads(kn_ref, vn_ref, True)
        for h in range(A_HEADS):
            o = _diff_finalize(lam_ref, g_ref, l_ref[h], acc_ref[h], tq, lam_init)
            o_ref[:, h * A_DV:(h + 1) * A_DV] = o.astype(o_ref.dtype)


def paged_diff_attention(qs, kr, proj, cache_k, cache_v, page_table, layer, lam_vec, norm_g, lam_init):
    bs, n_pages = page_table.shape
    n_pool = cache_k.shape[0]
    gw = GROUP_WIDTH
    tq = SAMPLE_Q_ROWS
    ck = cache_k.reshape(n_pool, DEPTH, PAGE_SIZE, gw)
    cv = cache_v.reshape(n_pool, DEPTH, PAGE_SIZE, gw)
    page = lambda bi, p, pt: (pt[bi * n_pages + jnp.minimum(p, n_pages - 1)], layer, 0, 0)
    const = lambda bi, p, pt: (0, 0)
    return pl.pallas_call(
        functools.partial(_paged_body, n_pages=n_pages, lam_init=lam_init), name="paged_diff_attn",
        grid_spec=pltpu.PrefetchScalarGridSpec(
            num_scalar_prefetch=1, grid=(bs, n_pages + 1),
            in_specs=[pl.BlockSpec((4, A_DH), const), pl.BlockSpec((1, A_DV), const),
                      pl.BlockSpec((tq, gw), lambda bi, p, pt: (bi * (SEQ_PAD // tq), 0)),
                      pl.BlockSpec((None, None, PAGE_SIZE, gw), page),
                      pl.BlockSpec((None, None, PAGE_SIZE, gw), page),
                      pl.BlockSpec((SEQ_PAD, gw), lambda bi, p, pt: (bi, 0)),
                      pl.BlockSpec((SEQ_PAD, gw), lambda bi, p, pt: (bi, COL_V // gw))],
            out_specs=pl.BlockSpec((tq, gw), lambda bi, p, pt: (bi, 0)),
            scratch_shapes=[pltpu.VMEM((A_HEADS, 2 * tq, A_DV), BF16), pltpu.VMEM((A_HEADS, 2 * tq, 1), F32),
                            pltpu.VMEM((A_HEADS, 2 * tq, 1), F32), pltpu.VMEM((A_HEADS, 2 * tq, A_DV), F32)]),
        out_shape=jax.ShapeDtypeStruct((bs * tq, gw), BF16),
        compiler_params=_params(("parallel", "arbitrary")),
    )(page_table.reshape(-1), lam_vec, norm_g.reshape(1, A_DV), qs, ck, cv, kr, proj)


def _ssd_body(z_ref, xbc_ref, dtr_ref, cw_ref, cb_ref, dtb_ref, alog_ref, dsk_ref, ng_ref, conv0_ref, ssm0_ref,
              o_ref, ssml_ref, win_ref, st_ref, *, nc, t_real):
    c = pl.program_id(1)
    q = SEQ_PAD
    gw = GROUP_WIDTH

    @pl.when(c == 0)
    def _():
        win_ref[0:8, :] = conv0_ref[...]
        st_ref[...] = ssm0_ref[...]

    win_ref[8:8 + q, :] = xbc_ref[...]
    conv = cb_ref[...]
    for j in range(B_CONV):
        conv = conv + win_ref[pl.ds(8 - (B_CONV - 1) + j, q), :] * cw_ref[j:j + 1, :]
    win_ref[0:8, :] = win_ref[q:q + 8, :]
    act = conv * _sigmoid(conv)
    xs = act[:, 0:gw]
    bmat = act[:, gw:gw + 2 * B_STATE]
    cmat = act[:, gw + 2 * B_STATE:gw + 4 * B_STATE]

    row = lax.broadcasted_iota(jnp.int32, (q, q), 0)
    lane = lax.broadcasted_iota(jnp.int32, (q, q), 1)
    dt = _softplus(dtr_ref[...] + dtb_ref[...])
    dt = jnp.where((c * q + row < t_real) & (lane < B_HEADS), dt, 0.0)
    dta = dt * (-jnp.exp(alog_ref[...]))
    causal = row >= lane
    acum = jnp.dot(causal.astype(F32), dta, precision=HIGHEST, preferred_element_type=F32)
    acum_t = acum.T
    dt_t = dt.T
    first_half_l = lane < B_HEADDIM
    first_half_r = row < B_HEADDIM

    y_pairs = []
    for g in range(2):
        cg = cmat[:, g * B_STATE:(g + 1) * B_STATE].astype(BF16)
        bg = bmat[:, g * B_STATE:(g + 1) * B_STATE].astype(BF16)
        cb = _dot_nt(cg, bg)
        for pp in range(2):
            p = 2 * g + pp
            xs_pair = xs[:, p * LANES:(p + 1) * LANES]
            st = st_ref[p]
            y_pair = jnp.zeros((q, LANES), F32)
            for hh in range(2):
                h = 2 * p + hh
                seg = acum[:, h:h + 1] - acum_t[h:h + 1, :]
                w = cb * jnp.exp(jnp.where(causal, seg, NEG_BIG)) * dt_t[h:h + 1, :]
                xh = jnp.where(first_half_l if hh == 0 else ~first_half_l, xs_pair, 0.0)
                y_pair = y_pair + _dot(w.astype(BF16), xh.astype(BF16))
            e_in = jnp.where(first_half_l, jnp.exp(acum[:, 2 * p:2 * p + 1]), jnp.exp(acum[:, 2 * p + 1:2 * p + 2]))
            y_pair = y_pair + e_in * _dot_nt(cg, st.astype(BF16))
            y_pairs.append(y_pair)
            al0 = acum[q - 1:q, 2 * p:2 * p + 1]
            al1 = acum[q - 1:q, 2 * p + 1:2 * p + 2]
            te0 = jnp.exp(al0 - acum_t[2 * p:2 * p + 1, :]) * dt_t[2 * p:2 * p + 1, :]
            te1 = jnp.exp(al1 - acum_t[2 * p + 1:2 * p + 2, :]) * dt_t[2 * p + 1:2 * p + 2, :]
            xw = xs_pair.T * jnp.where(first_half_r, te0, te1)
            st_ref[p] = jnp.where(first_half_r, jnp.exp(al0), jnp.exp(al1)) * st + _dot(xw.astype(BF16), bg)

    y = jnp.concatenate(y_pairs, axis=1) + xs * dsk_ref[...]
    z = z_ref[...]
    gt = y * (z * _sigmoid(z))
    half = gw // 2
    outs = []
    for g in range(2):
        gg = gt[:, g * half:(g + 1) * half]
        outs.append(gg * lax.rsqrt(jnp.mean(gg * gg, axis=-1, keepdims=True) + RMS_EPS))
    o_ref[...] = (jnp.concatenate(outs, axis=1) * ng_ref[...]).astype(o_ref.dtype)

    @pl.when(c == nc - 1)
    def _():
        ssml_ref[...] = st_ref[...]


def ssd_mixer(proj, dtr, conv0, ssm0, conv_w, conv_b, dt_bias, a_log, d_skip, norm_g, b, t, t_real):
    q = SEQ_PAD
    nc = t // q
    gw = GROUP_WIDTH
    rowblk = lambda width, col: pl.BlockSpec((q, width), lambda bi, c: (bi * nc + c, col))
    vec = lambda width: pl.BlockSpec((1, width), lambda bi, c: (0, 0))
    pad8 = lambda a: jnp.zeros((LANES,), F32).at[:B_HEADS].set(a).reshape(1, LANES)
    conv0p = jnp.pad(conv0, ((0, 0), (8 - (B_CONV - 1), 0), (0, 0)))
    ssm0p = ssm0.reshape(b, B_HEADS // 2, 2 * B_HEADDIM, B_STATE)
    o, ssml = pl.pallas_call(
        functools.partial(_ssd_body, nc=nc, t_real=t_real), name="ssd", grid=(b, nc),
        in_specs=[rowblk(gw, COL_Z // gw), rowblk(B_CONV_CH, COL_XBC // B_CONV_CH), rowblk(LANES, 0),
                  pl.BlockSpec((B_CONV, B_CONV_CH), lambda bi, c: (0, 0)), vec(B_CONV_CH), vec(LANES), vec(LANES),
                  vec(gw), vec(gw),
                  pl.BlockSpec((None, 8, B_CONV_CH), lambda bi, c: (bi, 0, 0)),
                  pl.BlockSpec((None, B_HEADS // 2, 2 * B_HEADDIM, B_STATE), lambda bi, c: (bi, 0, 0, 0))],
        out_specs=[pl.BlockSpec((q, gw), lambda bi, c: (bi * nc + c, 0)),
                   pl.BlockSpec((None, B_HEADS // 2, 2 * B_HEADDIM, B_STATE), lambda bi, c: (bi, 0, 0, 0))],
        out_shape=[jax.ShapeDtypeStruct((b * t, gw), BF16),
                   jax.ShapeDtypeStruct((b, B_HEADS // 2, 2 * B_HEADDIM, B_STATE), F32)],
        scratch_shapes=[pltpu.VMEM((8 + q, B_CONV_CH), F32),
                        pltpu.VMEM((B_HEADS // 2, 2 * B_HEADDIM, B_STATE), F32)],
        compiler_params=_params(("parallel", "arbitrary")),
    )(proj, proj, dtr, conv_w, conv_b.reshape(1, -1), pad8(dt_bias), pad8(a_log),
      jnp.repeat(d_skip, B_HEADDIM).reshape(1, gw), norm_g.reshape(1, gw), conv0p, ssm0p)
    return o, ssml.reshape(b, B_HEADS, B_HEADDIM, B_STATE)


def _rwkv_prep_body(r_ref, k_ref, v_ref, u_ref, sh_ref, vf_ref, mu_ref, w0_ref, w1_ref, w2_ref, a0_ref, a1_ref, a2_ref,
                    g1_ref, g2_ref, kk_ref, ka_ref, rk_ref, vmu_ref, v0_ref, v1_ref, v2_ref, seg_ref,
                    nkk_o, dr_o, d_o, b_o, km_o, v_o, br_o, kr_o, g_o, bon_o, carry_ref, *, first_layer):
    c = pl.program_id(1)
    gw = GROUP_WIDTH

    @pl.when(c == 0)
    def _():
        carry_ref[...] = sh_ref[...]

    row0 = lax.broadcasted_iota(jnp.int32, r_ref.shape, 0) == 0

    def delta(ref, i):
        x = ref[...]
        prev = jnp.where(row0, carry_ref[:, i * gw:(i + 1) * gw], pltpu.roll(x, 1, 0))
        carry_ref[:, i * gw:(i + 1) * gw] = x[x.shape[0] - 1:x.shape[0]]
        return x, prev - x

    r_in, d_r = delta(r_ref, 0)
    k_in, d_k = delta(k_ref, 1)
    v_in, d_v = delta(v_ref, 2)
    u_in, d_u = delta(u_ref, 3)
    mu = mu_ref[...]
    r = r_in + d_r * mu[0:1]
    k = k_in + d_k * mu[1:2]
    v = v_in + d_v * mu[2:3]
    xw = u_in + d_u * mu[3:4]
    xa = u_in + d_u * mu[4:5]
    xg = u_in + d_u * mu[5:6]

    def lora(x, a_ref, b_ref, act):
        return _dot(act(_dot(x.astype(BF16), a_ref[...])).astype(BF16), b_ref[...])

    w = -_softplus(-(w0_ref[...] + lora(xw, w1_ref, w2_ref, jnp.tanh))) - 0.5
    a = _sigmoid(a0_ref[...] + lora(xa, a1_ref, a2_ref, lambda t: t))
    g = lora(xg, g1_ref, g2_ref, _sigmoid)
    if not first_layer:
        xv = u_in + d_u * vmu_ref[...]
        v = v + (vf_ref[...] - v) * _sigmoid(v0_ref[...] + lora(xv, v1_ref, v2_ref, lambda t: t))
    seg = seg_ref[...]
    kk = k * kk_ref[...]
    kk = kk / jnp.maximum(jnp.sqrt(_seg_dot(kk * kk, seg)), 1e-12)
    km = k * (1.0 + (a - 1.0) * ka_ref[...])
    d = jnp.exp(-jnp.exp(w))
    bvec = kk * a
    nkk_o[...] = -kk
    dr_o[...] = d * r
    d_o[...] = d
    b_o[...] = bvec
    km_o[...] = km
    v_o[...] = v
    br_o[...] = _seg_dot(bvec * r, seg)
    kr_o[...] = _seg_dot(km * r, seg)
    g_o[...] = g
    bon_o[...] = _seg_dot(r * km * rk_ref[...], seg) * v


def _pad_cols(w, n):
    return jnp.pad(w, ((0, 0), (0, n - w.shape[1])))


def _pad_rows(w, n):
    return jnp.pad(w, ((0, n - w.shape[0]), (0, 0)))


def _head_seg_matrix(scale):
    h = jnp.arange(GROUP_WIDTH) // C_HEAD
    return ((h[:, None] == h[None, :]).astype(F32) * scale).astype(BF16)


def rwkv_prep(proj, shift0, v_first, lp, b, t, first_layer):
    q = SEQ_PAD
    nc = t // q
    gw = GROUP_WIDTH
    rowblk = lambda col: pl.BlockSpec((q, gw), lambda bi, c: (bi * nc + c, col))
    full = lambda a: pl.BlockSpec(a.shape, lambda bi, c: (0,) * a.ndim)
    row1 = lambda a: a.reshape(1, gw)
    lo = lambda a, bmat: (_pad_cols(a, LANES).astype(BF16), _pad_rows(bmat, LANES).astype(BF16))
    w1, w2 = lo(lp['w1'], lp['w2'])
    a1, a2 = lo(lp['a1'], lp['a2'])
    g1, g2 = lo(lp['g1'], lp['g2'])
    v1, v2 = lo(lp['v1'], lp['v2'])
    params = [lp['mu'], row1(lp['w0']), w1, w2, row1(lp['a0']), a1, a2, g1, g2, row1(lp['kk']), row1(lp['ka']),
              row1(lp['rk']), row1(lp['vmu']), row1(lp['v0']), v1, v2, _head_seg_matrix(1.0)]
    outs = pl.pallas_call(
        functools.partial(_rwkv_prep_body, first_layer=first_layer), name="rwkv_prep", grid=(b, nc),
        in_specs=[rowblk(COL_RKVU // gw), rowblk(COL_RKVU // gw + 1), rowblk(COL_RKVU // gw + 2),
                  rowblk(COL_RKVU // gw + 3), pl.BlockSpec((None, 1, 4 * gw), lambda bi, c: (bi, 0, 0)),
                  rowblk(0)] + [full(a) for a in params],
        out_specs=[rowblk(0)] * 10,
        out_shape=[jax.ShapeDtypeStruct((b * t, gw), F32)] * 10,
        scratch_shapes=[pltpu.VMEM((1, 4 * gw), F32)],
        compiler_params=_params(("parallel", "arbitrary")),
    )(proj, proj, proj, proj, shift0.reshape(b, 1, 4 * gw), v_first, *params)
    return outs


def _rwkv_scan_body(nkk_ref, dr_ref, d_ref, b_ref, km_ref, v_ref, br_ref, kr_ref, s0_ref, o_ref, sl_ref, st_ref,
                    *, nc, tc, t_real):
    c = pl.program_id(1)
    n = C_HEAD
    npairs = C_HEADS // 2

    @pl.when(c == 0)
    def _():
        st_ref[...] = s0_ref[...]

    @pl.when(c * tc >= t_real)
    def _():
        o_ref[...] = jnp.zeros_like(o_ref)

    @pl.when(c * tc < t_real)
    def _():
        lane = lax.broadcasted_iota(jnp.int32, (n, LANES), 1)
        first = lane < n
        zpad = jnp.zeros((LANES - tc, LANES), F32)

        def columns(ref, p):
            return jnp.concatenate([ref[:, p * LANES:(p + 1) * LANES], zpad], axis=0).T

        for p in range(npairs):
            cols = [columns(ref, p) for ref in (nkk_ref, dr_ref, d_ref, b_ref, km_ref)]
            st = st_ref[p]
            sl = slice(p * LANES, (p + 1) * LANES)
            for tt in range(tc):
                def colb(x):
                    col = x[:, tt:tt + 1]
                    return jnp.where(first, jnp.broadcast_to(col[0:n], (n, LANES)),
                                     jnp.broadcast_to(col[n:2 * n], (n, LANES)))
                nkk, drc, dc, bc, kc = (colb(x) for x in cols)
                sa = jnp.sum(st * nkk, axis=0, keepdims=True)
                u = jnp.sum(st * drc, axis=0, keepdims=True)
                vrow = v_ref[tt:tt + 1, sl]
                o_ref[tt:tt + 1, sl] = u + sa * br_ref[tt:tt + 1, sl] + vrow * kr_ref[tt:tt + 1, sl]
                st = st * dc + bc * sa + kc * vrow
            st_ref[p] = st

    @pl.when(c == nc - 1)
    def _():
        sl_ref[...] = st_ref[...]


def rwkv_scan(nkk, dr, d, bv, km, v, br, kr, s0, b, t, t_real):
    tc = 16 if t_real % 16 == 0 else 8
    assert t_real % tc == 0 and t % tc == 0
    nc = t // tc
    gw = GROUP_WIDTH
    n = C_HEAD
    rowblk = pl.BlockSpec((tc, gw), lambda bi, c: (bi * nc + c, 0))
    sblk = pl.BlockSpec((None, C_HEADS // 2, n, 2 * n), lambda bi, c: (bi, 0, 0, 0))
    s0t = s0.reshape(b, C_HEADS // 2, 2, n, n).transpose(0, 1, 4, 2, 3).reshape(b, C_HEADS // 2, n, 2 * n)
    o, sl = pl.pallas_call(
        functools.partial(_rwkv_scan_body, nc=nc, tc=tc, t_real=t_real), name="rwkv_scan", grid=(b, nc),
        in_specs=[rowblk] * 8 + [sblk],
        out_specs=[rowblk, sblk],
        out_shape=[jax.ShapeDtypeStruct((b * t, gw), F32), jax.ShapeDtypeStruct((b, C_HEADS // 2, n, 2 * n), F32)],
        scratch_shapes=[pltpu.VMEM((C_HEADS // 2, n, 2 * n), F32)],
        compiler_params=_params(("parallel", "arbitrary")),
    )(nkk, dr, d, bv, km, v, br, kr, s0t)
    s_last = sl.reshape(b, C_HEADS // 2, n, 2, n).transpose(0, 1, 3, 4, 2).reshape(b, C_HEADS, n, n)
    return o, s_last


def _rwkv_post_body(o_ref, g_ref, bon_ref, lg_ref, lb_ref, seg_ref, out_ref):
    o = o_ref[...]
    seg = seg_ref[...]
    xc = o - _seg_dot(o, seg)
    var = _seg_dot(xc * xc, seg)
    y = xc * lax.rsqrt(var + C_LN_EPS) * lg_ref[...] + lb_ref[...]
    out_ref[...] = ((y + bon_ref[...]) * g_ref[...]).astype(out_ref.dtype)


def rwkv_post(o, g, bonus, ln_g, ln_b):
    rows, gw = o.shape
    q = _tile(rows, 512, 16)
    rowblk = pl.BlockSpec((q, gw), lambda i: (i, 0))
    vec = pl.BlockSpec((1, gw), lambda i: (0, 0))
    return pl.pallas_call(
        _rwkv_post_body, name="rwkv_post", grid=(rows // q,),
        in_specs=[rowblk, rowblk, rowblk, vec, vec, pl.BlockSpec((gw, gw), lambda i: (0, 0))],
        out_specs=rowblk, out_shape=jax.ShapeDtypeStruct((rows, gw), BF16),
        compiler_params=_params(("parallel",)),
    )(o, g, bonus, ln_g.reshape(1, gw), ln_b.reshape(1, gw), _head_seg_matrix(1.0 / C_HEAD))


S5_LANES = D_GROUPS * D_STATE
S5_GB = GROUP_WIDTH // LANES


def _s5_body(u_ref, bre_ref, bim_ref, cre_ref, cim_ref, are_ref, aim_ref, pre_ref, pim_ref, dsk_ref, gw_ref, gb_ref,
             h0re_ref, h0im_ref, o_ref, hlre_ref, hlim_ref, hre_ref, him_ref, cr_ref, ci_ref, *, nc, t_real):
    c = pl.program_id(1)
    q = SEQ_PAD
    sb = S5_LANES // S5_GB

    @pl.when(c == 0)
    def _():
        cr_ref[...] = h0re_ref[...]
        ci_ref[...] = h0im_ref[...]

    u = u_ref[...]
    ub = u.astype(BF16)
    for gb in range(S5_GB):
        ug = ub[:, gb * LANES:(gb + 1) * LANES]
        hre_ref[:, gb * sb:(gb + 1) * sb] = _dot(ug, bre_ref[gb])
        him_ref[:, gb * sb:(gb + 1) * sb] = _dot(ug, bim_ref[gb])

    row8 = lax.broadcasted_iota(jnp.int32, (8, S5_LANES), 0)
    pre, pim = pre_ref[...], pim_ref[...]

    def group(i, carry):
        cr, ci = carry
        off = pl.multiple_of(i * 8, 8)
        xr = hre_ref[pl.ds(off, 8), :]
        xi = him_ref[pl.ds(off, 8), :]
        for k, sh in enumerate((1, 2, 4)):
            ar, ai = are_ref[k:k + 1, :], aim_ref[k:k + 1, :]
            sr = jnp.where(row8 >= sh, pltpu.roll(xr, sh, 0), 0.0)
            si = jnp.where(row8 >= sh, pltpu.roll(xi, sh, 0), 0.0)
            xr, xi = xr + ar * sr - ai * si, xi + ar * si + ai * sr
        xr, xi = xr + pre * cr - pim * ci, xi + pre * ci + pim * cr
        hre_ref[pl.ds(off, 8), :] = xr
        him_ref[pl.ds(off, 8), :] = xi
        return xr[7:8], xi[7:8]

    cr, ci = lax.fori_loop(0, q // 8, group, (cr_ref[...], ci_ref[...]))
    cr_ref[...] = cr
    ci_ref[...] = ci

    @pl.when(c == (t_real - 1) // q)
    def _():
        rl = (t_real - 1) % q
        hlre_ref[...] = hre_ref[rl:rl + 1, :]
        hlim_ref[...] = him_ref[rl:rl + 1, :]

    ys = []
    for gb in range(S5_GB):
        hr = hre_ref[:, gb * sb:(gb + 1) * sb].astype(BF16)
        hi = him_ref[:, gb * sb:(gb + 1) * sb].astype(BF16)
        ys.append(_dot(hr, cre_ref[gb]) - _dot(hi, cim_ref[gb]))
    y = jnp.concatenate(ys, axis=1) + dsk_ref[...] * u
    y = 0.5 * y * (1.0 + jnp.tanh(math.sqrt(2.0 / math.pi) * (y + 0.044715 * (y * y * y))))
    gate = _sigmoid(_dot(y.astype(BF16), gw_ref[...]) + gb_ref[...])
    o_ref[...] = (y * gate).astype(o_ref.dtype)


def s5_params(a_re, a_im, log_dt, b_re, b_im, c_re, c_im):
    step = jnp.exp(log_dt)[:, None]
    zr, zi = a_re * step, a_im * step

    def apow(k):
        e = jnp.exp(k * zr)
        return (e * jnp.cos(k * zi)).reshape(-1), (e * jnp.sin(k * zi)).reshape(-1)

    abr, abi = jnp.exp(zr) * jnp.cos(zi), jnp.exp(zr) * jnp.sin(zi)
    den = a_re * a_re + a_im * a_im
    fr = ((abr - 1.0) * a_re + abi * a_im) / den
    fi = (abi * a_re - (abr - 1.0) * a_im) / den
    bbr = fr[..., None] * b_re - fi[..., None] * b_im
    bbi = fr[..., None] * b_im + fi[..., None] * b_re
    gpb = D_GROUPS // S5_GB
    eye = jnp.eye(gpb, dtype=F32)

    def in_blocks(x):
        x = x.reshape(S5_GB, gpb, D_STATE, D_GROUP_CH)
        return jnp.einsum('bgnc,gh->bgchn', x, eye).reshape(S5_GB, gpb * D_GROUP_CH, gpb * D_STATE).astype(BF16)

    def out_blocks(x):
        x = x.reshape(S5_GB, gpb, D_GROUP_CH, D_STATE)
        return jnp.einsum('bgcn,gh->bgnhc', x, eye).reshape(S5_GB, gpb * D_STATE, gpb * D_GROUP_CH).astype(BF16)

    pows = [apow(float(k)) for k in (1, 2, 4)]
    p8 = [apow(float(k)) for k in range(1, 9)]
    return dict(bre=in_blocks(bbr), bim=in_blocks(bbi), cre=out_blocks(c_re), cim=out_blocks(c_im),
                are=jnp.stack([p[0] for p in pows]), aim=jnp.stack([p[1] for p in pows]),
                pre=jnp.stack([p[0] for p in p8]), pim=jnp.stack([p[1] for p in p8]))


def s5_mixer(proj, sp, d_skip, glu_w, glu_b, h0re, h0im, b, t, t_real):
    q = SEQ_PAD
    nc = t // q
    gw = GROUP_WIDTH
    full = lambda a: pl.BlockSpec(a.shape, lambda bi, c: (0,) * a.ndim)
    st = pl.BlockSpec((None, 1, S5_LANES), lambda bi, c: (bi, 0, 0))
    consts = [sp['bre'], sp['bim'], sp['cre'], sp['cim'], sp['are'], sp['aim'], sp['pre'], sp['pim'],
              d_skip.reshape(1, gw), glu_w.astype(BF16), glu_b.reshape(1, gw)]
    o, hlre, hlim = pl.pallas_call(
        functools.partial(_s5_body, nc=nc, t_real=t_real), name="s5", grid=(b, nc),
        in_specs=[pl.BlockSpec((q, gw), lambda bi, c: (bi * nc + c, COL_S5 // gw))] + [full(a) for a in consts]
        + [st, st],
        out_specs=[pl.BlockSpec((q, gw), lambda bi, c: (bi * nc + c, 0)), st, st],
        out_shape=[jax.ShapeDtypeStruct((b * t, gw), BF16), jax.ShapeDtypeStruct((b, 1, S5_LANES), F32),
                   jax.ShapeDtypeStruct((b, 1, S5_LANES), F32)],
        scratch_shapes=[pltpu.VMEM((q, S5_LANES), F32), pltpu.VMEM((q, S5_LANES), F32),
                        pltpu.VMEM((1, S5_LANES), F32), pltpu.VMEM((1, S5_LANES), F32)],
        compiler_params=_params(("parallel", "arbitrary")),
    )(proj, *consts, h0re.reshape(b, 1, S5_LANES), h0im.reshape(b, 1, S5_LANES))
    return o, hlre.reshape(b, D_GROUPS, D_STATE), hlim.reshape(b, D_GROUPS, D_STATE)


def _pad_seq(a, b, t, t_pad):
    return jnp.pad(a.reshape(b, t, -1), ((0, 0), (0, t_pad - t), (0, 0))).reshape(b * t_pad, -1)


def kernel(x_prompt, x_sample, cache_k, cache_v, state_ssm, state_conv, state_wkv, state_shift, state_s5_re, state_s5_im, page_table, meta_tokens, ln_in_g, ln_in_b, w_in, w_out, ln1_g, ln1_b, ln2_g, ln2_b, a_lambda, a_norm_g, b_conv_w, b_conv_b, b_dt_bias, b_a_log, b_d, b_norm_g, c_mu, c_w0, c_w1, c_w2, c_a0, c_a1, c_a2, c_v_mu, c_v0, c_v1, c_v2, c_g1, c_g2, c_kk, c_ka, c_rk, c_ln_g, c_ln_b, d_a_re, d_a_im, d_log_dt, d_b_re, d_b_im, d_c_re, d_c_im, d_d, d_glu_w, d_glu_b, f_w1, f_w3, f_w2, m_router, m_w1, m_w3, m_w2):
    bp, seq, d = x_prompt.shape
    bs, ts, _ = x_sample.shape
    tp = seq + N_META
    tpp = -(-tp // SEQ_PAD) * SEQ_PAD
    tsp = SEQ_PAD
    rp = bp * tpp
    past_len = page_table.shape[1] * PAGE_SIZE
    gw = GROUP_WIDTH

    meta = jnp.broadcast_to(meta_tokens[None], (bp, N_META, d))
    xp = jnp.pad(jnp.concatenate([meta, x_prompt], axis=1), ((0, 0), (0, tpp - tp), (0, 0)))
    x0 = jnp.concatenate([xp.reshape(rp, d), x_sample.reshape(bs * ts, d)], axis=0)
    x, xb = ln_in(x0, ln_in_g, ln_in_b)

    dt0 = COL_RKVU
    w_main = jnp.concatenate([w_in[:, :, :dt0], w_in[:, :, dt0 + B_HEADS:]], axis=2)
    w_dt = jnp.pad(w_in[:, :, dt0:dt0 + B_HEADS], ((0, 0), (0, 0), (0, LANES - B_HEADS)))
    router_w = jnp.pad(m_router, ((0, 0), (0, 0), (0, LANES - N_EXPERTS)))

    tabs_p = _rope_tables(jnp.arange(tpp, dtype=jnp.int32))
    tabs_s = _rope_tables(past_len + jnp.arange(tsp, dtype=jnp.int32))

    zeros = lambda *s: jnp.zeros(s, F32)
    groups = [
        dict(b=bp, t=tpp, tr=tp, conv=zeros(bp, DEPTH, B_CONV - 1, B_CONV_CH),
             ssm=zeros(bp, DEPTH, B_HEADS, B_HEADDIM, B_STATE), shift=zeros(bp, DEPTH, 4 * gw),
             wkv=zeros(bp, DEPTH, C_HEADS, C_HEAD, C_HEAD), s5re=zeros(bp, DEPTH, D_GROUPS, D_STATE),
             s5im=zeros(bp, DEPTH, D_GROUPS, D_STATE), tabs=tabs_p),
        dict(b=bs, t=tsp, tr=ts, conv=state_conv, ssm=state_ssm, shift=state_shift, wkv=state_wkv,
             s5re=state_s5_re, s5im=state_s5_im, tabs=tabs_s),
    ]
    outs = [dict(k=[], v=[], ssm=[], conv=[], wkv=[], shift=[], s5re=[], s5im=[]) for _ in groups]
    v_first = [None, None]

    for l in range(DEPTH):
        lam_init = 0.8 - 0.6 * math.exp(-0.3 * l)
        proj_all = matmul_layer(xb, w_main, l, 512, "proj_in")
        dtr_all = matmul_layer(xb, w_dt, l, LANES, "proj_dt")
        sp = s5_params(d_a_re[l], d_a_im[l], d_log_dt[l], d_b_re[l], d_b_im[l], d_c_re[l], d_c_im[l])
        lv = max(l - 1, 0)
        lp = dict(mu=c_mu[l], w0=c_w0[l], w1=c_w1[l], w2=c_w2[l], a0=c_a0[l], a1=c_a1[l], a2=c_a2[l], g1=c_g1[l],
                  g2=c_g2[l], kk=c_kk[l], ka=c_ka[l], rk=c_rk[l].reshape(-1), vmu=c_v_mu[lv], v0=c_v0[lv],
                  v1=c_v1[lv], v2=c_v2[lv])
        mixes = []
        for gi, (gr, out) in enumerate(zip(groups, outs)):
            b, t, tr = gr['b'], gr['t'], gr['tr']
            if gi == 0:
                proj, dtr = proj_all, dtr_all
            else:
                proj = _pad_seq(proj_all[rp:], b, tr, t)
                dtr = _pad_seq(dtr_all[rp:], b, tr, t)
            qs, kr = rope_qk(proj, gr['tabs'], b, t)
            if gi == 0:
                o_a = flash_diff_attention(qs, kr, proj, a_lambda[l], a_norm_g[l], b, t, lam_init)
            else:
                o_a = paged_diff_attention(qs, kr, proj, cache_k, cache_v, page_table, l, a_lambda[l], a_norm_g[l],
                                           lam_init)
                o_a = _pad_seq(o_a.reshape(b, SAMPLE_Q_ROWS, gw)[:, :tr], b, tr, t)
            o_b, ssm_new = ssd_mixer(proj, dtr, gr['conv'][:, l], gr['ssm'][:, l], b_conv_w[l], b_conv_b[l],
                                     b_dt_bias[l], b_a_log[l], b_d[l], b_norm_g[l], b, t, tr)
            vf = proj[:, :gw] if l == 0 else v_first[gi]
            nkk, dr, dd, bv, km, v, br, krr, g, bonus = rwkv_prep(proj, gr['shift'][:, l], vf, lp, b, t, l == 0)
            if l == 0:
                v_first[gi] = v
            o_scan, wkv_new = rwkv_scan(nkk, dr, dd, bv, km, v, br, krr, gr['wkv'][:, l], b, t, tr)
            o_c = rwkv_post(o_scan, g, bonus, c_ln_g[l], c_ln_b[l])
            o_d, s5re_new, s5im_new = s5_mixer(proj, sp, d_d[l], d_glu_w[l], d_glu_b[l], gr['s5re'][:, l],
                                               gr['s5im'][:, l], b, t, tr)
            mix = jnp.concatenate([o_a, o_b, o_c, o_d], axis=1)
            mixes.append(mix if gi == 0 else mix.reshape(b, t, 4 * gw)[:, :tr].reshape(b * tr, 4 * gw))
            p3 = proj[:b * t].reshape(b, t, -1)
            out['k'].append(kr.reshape(b, t, A_HEADS, A_DV)[:, :tr])
            out['v'].append(p3[:, :tr, COL_V:COL_V + gw].reshape(b, tr, A_HEADS, A_DV))
            out['ssm'].append(ssm_new)
            out['conv'].append(p3[:, tr - (B_CONV - 1):tr, COL_XBC:COL_XBC + B_CONV_CH])
            out['wkv'].append(wkv_new)
            out['shift'].append(p3[:, tr - 1, COL_RKVU:COL_RKVU + 4 * gw])
            out['s5re'].append(s5re_new)
            out['s5im'].append(s5im_new)
        mix_all = jnp.concatenate(mixes, axis=0)
        x, xb = matmul_resid_ln(mix_all, w_out, l, x, ln1_g[l], ln1_b[l], "proj_out")
        if l % 2 == 0:
            hb = swiglu_up(xb, f_w1, f_w3, l // 2)
            x, xb = matmul_resid_ln(hb, f_w2, l // 2, x, ln2_g[l], ln2_b[l], "ffn_down")
        else:
            x, xb = moe_layer(x, xb, router_w, m_w1, m_w3, m_w2, l // 2, ln2_g[l], ln2_b[l])

    y_p = x[:rp].reshape(bp, tpp, d)[:, N_META:tp]
    y_s = x[rp:].reshape(bs, ts, d)
    st = lambda gi, key: jnp.stack(outs[gi][key], axis=1)
    return (y_p, y_s, st(0, 'k'), st(0, 'v'), st(1, 'k'), st(1, 'v'), st(0, 'ssm'), st(1, 'ssm'),
            st(0, 'conv'), st(1, 'conv'), st(0, 'wkv'), st(1, 'wkv'), st(0, 'shift'), st(1, 'shift'),
            st(0, 's5re'), st(0, 's5im'), st(1, 's5re'), st(1, 's5im'))
```

```python
import functools
import math

import jax
import jax.numpy as jnp
from jax import lax
from jax.experimental import pallas as pl
from jax.experimental.pallas import tpu as pltpu

F32 = jnp.float32
BF16 = jnp.bfloat16
HIGHEST = lax.Precision.HIGHEST

D_MODEL = 2048
DEPTH = 4
PAGE_SIZE = 128
N_META = 16
GROUP_WIDTH = 512
A_HEADS = 4
A_DH = 64
A_DV = 128
ROT_DIM = 16
ROPE_THETA = 500000.0
B_HEADS = 8
B_HEADDIM = 64
B_STATE = 128
B_CONV = 4
B_CONV_CH = 1024
C_HEADS = 8
C_HEAD = 64
C_LN_EPS = 64e-5
D_GROUPS = 32
D_GROUP_CH = 16
D_STATE = 64
FF_DIM = 5632
N_EXPERTS = 8
TOP_K = 2
ALPHA = (2.0 * DEPTH) ** 0.25
LN_EPS = 1e-5
RMS_EPS = 1e-5
NEG_BIG = -1e30

SEQ_PAD = 128
LANES = 128
V7X_VMEM_LIMIT = 56 * 1024 * 1024
MOE_BLOCK_ROWS = 512
N_PROJ = 11 * GROUP_WIDTH
COL_Q, COL_K, COL_V, COL_Z, COL_XBC, COL_RKVU, COL_S5 = 0, 512, 1024, 1536, 2048, 3072, 5120


def _tile(n, target, mult):
    best = None
    for d in range(mult, min(n, target) + 1, mult):
        if n % d == 0:
            best = d
    return n if best is None else best


def _params(sem, vmem=None):
    return pltpu.CompilerParams(dimension_semantics=sem, vmem_limit_bytes=vmem)


def _sigmoid(x):
    return 1.0 / (1.0 + jnp.exp(-x))


def _softplus(x):
    return jnp.maximum(x, 0.0) + jnp.log(1.0 + jnp.exp(-jnp.abs(x)))


def _ln_rows(x, g, b):
    mu = jnp.mean(x, axis=-1, keepdims=True)
    xc = x - mu
    var = jnp.mean(xc * xc, axis=-1, keepdims=True)
    return xc * lax.rsqrt(var + LN_EPS) * g + b


def _dot(a, b):
    return jnp.dot(a, b, preferred_element_type=F32)


def _dot_nt(a, b):
    return lax.dot_general(a, b, (((1,), (1,)), ((), ())), preferred_element_type=F32)


def _seg_dot(x, p):
    hi = x.astype(BF16)
    lo = (x - hi.astype(F32)).astype(BF16)
    return _dot(hi, p) + _dot(lo, p)


def _ln_in_body(x_ref, g_ref, b_ref, o_ref, ob_ref):
    y = _ln_rows(x_ref[...], g_ref[...], b_ref[...])
    o_ref[...] = y
    ob_ref[...] = y.astype(BF16)


def ln_in(x, g, b):
    m, d = x.shape
    tm = _tile(m, 512, 16)
    return pl.pallas_call(
        _ln_in_body, name="ln_in", grid=(m // tm,),
        in_specs=[pl.BlockSpec((tm, d), lambda i: (i, 0)), pl.BlockSpec((1, d), lambda i: (0, 0)),
                  pl.BlockSpec((1, d), lambda i: (0, 0))],
        out_specs=[pl.BlockSpec((tm, d), lambda i: (i, 0)), pl.BlockSpec((tm, d), lambda i: (i, 0))],
        out_shape=[jax.ShapeDtypeStruct((m, d), F32), jax.ShapeDtypeStruct((m, d), BF16)],
        compiler_params=_params(("parallel",)),
    )(x, g.reshape(1, d), b.reshape(1, d))


def _mm_body(x_ref, w_ref, o_ref):
    o_ref[...] = _dot(x_ref[...], w_ref[...].astype(BF16))


def matmul_layer(xb, w, layer, tn, name):
    m, k = xb.shape
    n = w.shape[2]
    tm = _tile(m, 1280, 16)
    return pl.pallas_call(
        _mm_body, name=name, grid=(m // tm, n // tn),
        in_specs=[pl.BlockSpec((tm, k), lambda i, j: (i, 0)),
                  pl.BlockSpec((None, k, tn), lambda i, j: (layer, 0, j))],
        out_specs=pl.BlockSpec((tm, tn), lambda i, j: (i, j)),
        out_shape=jax.ShapeDtypeStruct((m, n), F32),
        compiler_params=_params(("parallel", "arbitrary"), V7X_VMEM_LIMIT),
    )(xb, w)


def _router_body(x_ref, w_ref, o_ref):
    o_ref[...] = jnp.dot(x_ref[...], w_ref[...], precision=HIGHEST, preferred_element_type=F32)


def router_logits(x, w, layer):
    m, k = x.shape
    n = w.shape[2]
    tm = _tile(m, 512, 8)
    return pl.pallas_call(
        _router_body, name="router", grid=(m // tm,),
        in_specs=[pl.BlockSpec((tm, k), lambda i: (i, 0)), pl.BlockSpec((None, k, n), lambda i: (layer, 0, 0))],
        out_specs=pl.BlockSpec((tm, n), lambda i: (i, 0)),
        out_shape=jax.ShapeDtypeStruct((m, n), F32),
        compiler_params=_params(("parallel",), V7X_VMEM_LIMIT),
    )(x, w)


def _glu_body(x_ref, w1_ref, w3_ref, o_ref):
    x = x_ref[...]
    a = _dot(x, w1_ref[...].astype(BF16))
    b = _dot(x, w3_ref[...].astype(BF16))
    o_ref[...] = (a * _sigmoid(a) * b).astype(BF16)


def swiglu_up(xb, w1, w3, layer):
    m, k = xb.shape
    n = w1.shape[2]
    tm = _tile(m, 1280, 16)
    tn = 512
    wspec = pl.BlockSpec((None, k, tn), lambda i, j: (layer, 0, j))
    return pl.pallas_call(
        _glu_body, name="swiglu_up", grid=(m // tm, n // tn),
        in_specs=[pl.BlockSpec((tm, k), lambda i, j: (i, 0)), wspec, wspec],
        out_specs=pl.BlockSpec((tm, tn), lambda i, j: (i, j)),
        out_shape=jax.ShapeDtypeStruct((m, n), BF16),
        compiler_params=_params(("parallel", "arbitrary"), V7X_VMEM_LIMIT),
    )(xb, w1, w3)


def _mm_ln_body(h_ref, w_ref, r_ref, g_ref, b_ref, o_ref, ob_ref, acc_ref, *, nk):
    k = pl.program_id(1)

    @pl.when(k == 0)
    def _():
        acc_ref[...] = jnp.zeros_like(acc_ref)

    acc_ref[...] += _dot(h_ref[...], w_ref[...].astype(BF16))

    @pl.when(k == nk - 1)
    def _():
        y = _ln_rows(ALPHA * r_ref[...] + acc_ref[...], g_ref[...], b_ref[...])
        o_ref[...] = y
        ob_ref[...] = y.astype(BF16)


def matmul_resid_ln(hb, w, layer, resid, g, b, name):
    m, k = hb.shape
    n = w.shape[2]
    tm = _tile(m, 640, 16)
    tk = 512
    nk = k // tk
    row = pl.BlockSpec((tm, n), lambda i, kk: (i, 0))
    vec = pl.BlockSpec((1, n), lambda i, kk: (0, 0))
    return pl.pallas_call(
        functools.partial(_mm_ln_body, nk=nk), name=name, grid=(m // tm, nk),
        in_specs=[pl.BlockSpec((tm, tk), lambda i, kk: (i, kk)),
                  pl.BlockSpec((None, tk, n), lambda i, kk: (layer, kk, 0)), row, vec, vec],
        out_specs=[row, row],
        out_shape=[jax.ShapeDtypeStruct((m, n), F32), jax.ShapeDtypeStruct((m, n), BF16)],
        scratch_shapes=[pltpu.VMEM((tm, n), F32)],
        compiler_params=_params(("parallel", "arbitrary"), V7X_VMEM_LIMIT),
    )(hb, w, resid, g.reshape(1, n), b.reshape(1, n))


def _gather_body(idx_ref, src_ref, o_ref, buf, sem, *, br):
    base = pl.program_id(0) * br

    def row_copy(r, src_row):
        return pltpu.make_async_copy(src_ref.at[pl.ds(src_row, 1)], buf.at[pl.ds(r, 1)], sem)

    def start(r, c):
        row_copy(r, idx_ref[base + r]).start()
        return c

    def wait(r, c):
        row_copy(r, 0).wait()
        return c

    lax.fori_loop(0, br, start, 0)
    lax.fori_loop(0, br, wait, 0)
    o_ref[...] = buf[...].astype(BF16)


def gather_rows_bf16(src, idx):
    r = idx.shape[0]
    d = src.shape[1]
    br = _tile(r, 256, 16)
    return pl.pallas_call(
        functools.partial(_gather_body, br=br), name="moe_gather",
        grid_spec=pltpu.PrefetchScalarGridSpec(
            num_scalar_prefetch=1, grid=(r // br,),
            in_specs=[pl.BlockSpec(memory_space=pl.ANY)],
            out_specs=pl.BlockSpec((br, d), lambda i, idx_ref: (i, 0)),
            scratch_shapes=[pltpu.VMEM((br, d), F32), pltpu.SemaphoreType.DMA(())]),
        out_shape=jax.ShapeDtypeStruct((r, d), BF16),
        compiler_params=_params(("arbitrary",)),
    )(idx, src)


def _moe_up_body(be_ref, bv_ref, x_ref, w1_ref, w3_ref, o_ref):
    @pl.when(bv_ref[pl.program_id(1)] > 0)
    def _():
        _glu_body(x_ref, w1_ref, w3_ref, o_ref)

    @pl.when(bv_ref[pl.program_id(1)] == 0)
    def _():
        o_ref[...] = jnp.zeros_like(o_ref)


def moe_up(xg, w1, w3, mi, block_exp, block_valid, bm):
    r, k = xg.shape
    n = w1.shape[3]
    tn = 512
    wspec = pl.BlockSpec((None, None, k, tn), lambda j, i, be, bv: (mi, be[i], 0, j))
    return pl.pallas_call(
        _moe_up_body, name="moe_up",
        grid_spec=pltpu.PrefetchScalarGridSpec(
            num_scalar_prefetch=2, grid=(n // tn, r // bm),
            in_specs=[pl.BlockSpec((bm, k), lambda j, i, be, bv: (i, 0)), wspec, wspec],
            out_specs=pl.BlockSpec((bm, tn), lambda j, i, be, bv: (i, j))),
        out_shape=jax.ShapeDtypeStruct((r, n), BF16),
        compiler_params=_params(("arbitrary", "arbitrary"), V7X_VMEM_LIMIT),
    )(block_exp, block_valid, xg, w1, w3)


def _moe_down_body(be_ref, bv_ref, h_ref, w_ref, o_ref):
    @pl.when(bv_ref[pl.program_id(1)] > 0)
    def _():
        _mm_body(h_ref, w_ref, o_ref)

    @pl.when(bv_ref[pl.program_id(1)] == 0)
    def _():
        o_ref[...] = jnp.zeros_like(o_ref)


def moe_down(hg, w2, mi, block_exp, block_valid, bm):
    r, k = hg.shape
    n = w2.shape[3]
    tn = 256
    return pl.pallas_call(
        _moe_down_body, name="moe_down",
        grid_spec=pltpu.PrefetchScalarGridSpec(
            num_scalar_prefetch=2, grid=(n // tn, r // bm),
            in_specs=[pl.BlockSpec((bm, k), lambda j, i, be, bv: (i, 0)),
                      pl.BlockSpec((None, None, k, tn), lambda j, i, be, bv: (mi, be[i], 0, j))],
            out_specs=pl.BlockSpec((bm, tn), lambda j, i, be, bv: (i, j))),
        out_shape=jax.ShapeDtypeStruct((r, n), F32),
        compiler_params=_params(("arbitrary", "arbitrary"), V7X_VMEM_LIMIT),
    )(block_exp, block_valid, hg, w2)


def _combine_body(d0_ref, d1_ref, y_ref, g0_ref, g1_ref, r_ref, g_ref, b_ref, o_ref, ob_ref, buf0, buf1, sems, *, tm):
    base = pl.program_id(0) * tm

    def row_copy(r, src_row, buf, k):
        return pltpu.make_async_copy(y_ref.at[pl.ds(src_row, 1)], buf.at[pl.ds(r, 1)], sems.at[k])

    def start(r, c):
        row_copy(r, d0_ref[base + r], buf0, 0).start()
        row_copy(r, d1_ref[base + r], buf1, 1).start()
        return c

    def wait(r, c):
        row_copy(r, 0, buf0, 0).wait()
        row_copy(r, 0, buf1, 1).wait()
        return c

    lax.fori_loop(0, tm, start, 0)
    lax.fori_loop(0, tm, wait, 0)
    f = g0_ref[...] * buf0[...] + g1_ref[...] * buf1[...]
    y = _ln_rows(ALPHA * r_ref[...] + f, g_ref[...], b_ref[...])
    o_ref[...] = y
    ob_ref[...] = y.astype(BF16)


def moe_combine_ln(y_rows, d0, d1, g0, g1, resid, g, b):
    m, n = resid.shape
    tm = _tile(m, 256, 16)
    row = pl.BlockSpec((tm, n), lambda i, a, c: (i, 0))
    col = pl.BlockSpec((tm, 1), lambda i, a, c: (i, 0))
    vec = pl.BlockSpec((1, n), lambda i, a, c: (0, 0))
    return pl.pallas_call(
        functools.partial(_combine_body, tm=tm), name="moe_combine",
        grid_spec=pltpu.PrefetchScalarGridSpec(
            num_scalar_prefetch=2, grid=(m // tm,),
            in_specs=[pl.BlockSpec(memory_space=pl.ANY), col, col, row, vec, vec],
            out_specs=[row, row],
            scratch_shapes=[pltpu.VMEM((tm, n), F32), pltpu.VMEM((tm, n), F32), pltpu.SemaphoreType.DMA((2,))]),
        out_shape=[jax.ShapeDtypeStruct((m, n), F32), jax.ShapeDtypeStruct((m, n), BF16)],
        compiler_params=_params(("arbitrary",), V7X_VMEM_LIMIT),
    )(d0, d1, y_rows, g0, g1, resid, g.reshape(1, n), b.reshape(1, n))


def moe_layer(x, xb, router_w, w1, w3, w2, mi, ln_g, ln_b):
    m = x.shape[0]
    bm = MOE_BLOCK_ROWS
    logits = router_logits(x, router_w, mi)[:, :N_EXPERTS]
    eidx = lax.broadcasted_iota(jnp.int32, logits.shape, 1)
    v1 = jnp.max(logits, axis=-1, keepdims=True)
    i1 = jnp.min(jnp.where(logits == v1, eidx, N_EXPERTS), axis=-1, keepdims=True)
    rest = jnp.where(eidx == i1, -jnp.inf, logits)
    v2 = jnp.max(rest, axis=-1, keepdims=True)
    i2 = jnp.min(jnp.where(rest == v2, eidx, N_EXPERTS), axis=-1, keepdims=True)
    top_i = jnp.concatenate([i1, i2], axis=1)
    gates = jax.nn.softmax(jnp.concatenate([v1, v2], axis=1), axis=-1)
    e_flat = top_i.reshape(-1)
    n_assign = m * TOP_K
    onehot = (e_flat[:, None] == jnp.arange(N_EXPERTS, dtype=e_flat.dtype)[None, :]).astype(jnp.int32)
    rank = jnp.take_along_axis(jnp.cumsum(onehot, axis=0) - onehot, e_flat[:, None], axis=1)[:, 0]
    counts = jnp.sum(onehot, axis=0)
    padded = (counts + bm - 1) // bm * bm
    pad_end = jnp.cumsum(padded)
    pad_start = pad_end - padded
    dest = (pad_start[e_flat] + rank).astype(jnp.int32)
    n_blocks = -(-(n_assign + N_EXPERTS * (bm - 1)) // bm)
    rows = n_blocks * bm
    row_tok = jnp.zeros((rows,), jnp.int32).at[dest].set(jnp.arange(n_assign, dtype=jnp.int32) // TOP_K)
    blk_start = jnp.arange(n_blocks, dtype=jnp.int32) * bm
    block_exp = jnp.minimum(jnp.sum((pad_end[None, :] <= blk_start[:, None]).astype(jnp.int32), axis=1),
                            N_EXPERTS - 1)
    block_valid = (blk_start < pad_end[-1]).astype(jnp.int32)
    xg = gather_rows_bf16(x, row_tok)
    hg = moe_up(xg, w1, w3, mi, block_exp, block_valid, bm)
    yg = moe_down(hg, w2, mi, block_exp, block_valid, bm)
    dest2 = dest.reshape(m, TOP_K)
    return moe_combine_ln(yg, dest2[:, 0], dest2[:, 1], gates[:, 0:1], gates[:, 1:2], x, ln_g, ln_b)


def _rope_tables(pos):
    half = ROT_DIM // 2
    inv = ROPE_THETA ** (-2.0 * jnp.arange(half, dtype=F32) / ROT_DIM)
    ang = pos.astype(F32)[:, None] * inv[None, :]
    cos, sin = jnp.cos(ang), jnp.sin(ang)
    t = pos.shape[0]
    z8 = jnp.zeros((t, half), F32)
    z48 = jnp.zeros((t, A_DH - ROT_DIM), F32)
    c64 = jnp.concatenate([cos, cos, jnp.ones((t, A_DH - ROT_DIM), F32)], axis=1)
    s_up = jnp.concatenate([z8, sin, z48], axis=1)
    s_dn = jnp.concatenate([-sin, z8, z48], axis=1)
    return tuple(jnp.tile(a, (1, LANES // A_DH)) for a in (c64, s_up, s_dn))


def _rope_body(q_ref, k_ref, c_ref, su_ref, sd_ref, qo_ref, ko_ref):
    c, su, sd = c_ref[...], su_ref[...], sd_ref[...]
    half = ROT_DIM // 2
    for g in range(GROUP_WIDTH // LANES):
        sl = slice(g * LANES, (g + 1) * LANES)
        for src, dst, scale in ((q_ref, qo_ref, 1.0 / math.sqrt(A_DH)), (k_ref, ko_ref, 1.0)):
            x = src[:, sl]
            y = x * c + pltpu.roll(x, half, 1) * su + pltpu.roll(x, LANES - half, 1) * sd
            dst[:, sl] = (y * scale).astype(dst.dtype)


def rope_qk(proj, tables, b, t):
    rows = b * t
    tq = _tile(t, 1024, SEQ_PAD)
    nq = t // tq
    gw = GROUP_WIDTH
    tab = pl.BlockSpec((tq, LANES), lambda i: (i % nq, 0))
    return pl.pallas_call(
        _rope_body, name="rope", grid=(rows // tq,),
        in_specs=[pl.BlockSpec((tq, gw), lambda i: (i, COL_Q // gw)),
                  pl.BlockSpec((tq, gw), lambda i: (i, COL_K // gw)), tab, tab, tab],
        out_specs=[pl.BlockSpec((tq, gw), lambda i: (i, 0)), pl.BlockSpec((tq, gw), lambda i: (i, 0))],
        out_shape=[jax.ShapeDtypeStruct((rows, gw), BF16), jax.ShapeDtypeStruct((rows, gw), F32)],
        compiler_params=_params(("parallel",)),
    )(proj, proj, *tables)


def _stack_maps(q):
    lane = lax.broadcasted_iota(jnp.int32, q.shape, 1)
    zero = jnp.zeros_like(q)
    return jnp.concatenate([jnp.where(lane < A_DH, q, zero), jnp.where(lane >= A_DH, q, zero)], axis=0)


def _diff_finalize(lam_ref, g_ref, l, acc, tq, lam_init):
    lv = lam_ref[...]
    lam = (jnp.exp(jnp.sum(lv[0:1] * lv[1:2], axis=-1, keepdims=True))
           - jnp.exp(jnp.sum(lv[2:3] * lv[3:4], axis=-1, keepdims=True)) + lam_init)
    o = acc[0:tq] / l[0:tq] - lam * (acc[tq:2 * tq] / l[tq:2 * tq])
    o = o * lax.rsqrt(jnp.mean(o * o, axis=-1, keepdims=True) + RMS_EPS) * g_ref[...] * (1.0 - lam_init)
    return o


def _flash_body(lam_ref, g_ref, q_ref, k_ref, v_ref, o_ref, qq_ref, m_ref, acc_ref, *, tq, nk, lam_init):
    qi = pl.program_id(1)
    ki = pl.program_id(2)
    dv = A_DV

    @pl.when(ki == 0)
    def _():
        for h in range(A_HEADS):
            qq_ref[h] = _stack_maps(q_ref[:, h * dv:(h + 1) * dv])
        m_ref[...] = jnp.full_like(m_ref, NEG_BIG)
        acc_ref[...] = jnp.zeros_like(acc_ref)

    def step(diagonal):
        ones = jnp.ones((tq, dv), BF16)
        for h in range(A_HEADS):
            sl = slice(h * dv, (h + 1) * dv)
            s = _dot_nt(qq_ref[h], k_ref[:, sl].astype(BF16))
            if diagonal:
                r = lax.broadcasted_iota(jnp.int32, s.shape, 0)
                r = jnp.where(r >= tq, r - tq, r)
                c = lax.broadcasted_iota(jnp.int32, s.shape, 1)
                s = jnp.where(c <= r, s, NEG_BIG)
            m_prev = m_ref[h]
            m_new = jnp.maximum(m_prev, jnp.max(s, axis=-1, keepdims=True))
            a = jnp.exp(m_prev - m_new)
            p = jnp.exp(s - m_new).astype(BF16)
            v1 = jnp.concatenate([v_ref[:, sl].astype(BF16), ones], axis=1)
            acc_ref[h] = a * acc_ref[h] + _dot(p, v1)
            m_ref[h] = m_new

    @pl.when(ki < qi)
    def _():
        step(False)

    @pl.when(ki == qi)
    def _():
        step(True)

    @pl.when(ki == nk - 1)
    def _():
        for h in range(A_HEADS):
            acc = acc_ref[h]
            o = _diff_finalize(lam_ref, g_ref, acc[:, dv:2 * dv], acc[:, 0:dv], tq, lam_init)
            o_ref[:, h * dv:(h + 1) * dv] = o.astype(o_ref.dtype)


def flash_diff_attention(qs, kr, proj, lam_vec, norm_g, b, t, lam_init):
    tq = _tile(t, 512, SEQ_PAD)
    nq = t // tq
    gw = GROUP_WIDTH
    kv_map = lambda col: (lambda bi, qi, ki: (bi * nq + jnp.minimum(ki, qi), col))
    qmap = lambda bi, qi, ki: (bi * nq + qi, 0)
    return pl.pallas_call(
        functools.partial(_flash_body, tq=tq, nk=nq, lam_init=lam_init), name="flash_diff_attn",
        grid=(b, nq, nq),
        in_specs=[pl.BlockSpec((4, A_DH), lambda bi, qi, ki: (0, 0)),
                  pl.BlockSpec((1, A_DV), lambda bi, qi, ki: (0, 0)),
                  pl.BlockSpec((tq, gw), qmap), pl.BlockSpec((tq, gw), kv_map(0)),
                  pl.BlockSpec((tq, gw), kv_map(COL_V // gw))],
        out_specs=pl.BlockSpec((tq, gw), qmap),
        out_shape=jax.ShapeDtypeStruct((b * t, gw), BF16),
        scratch_shapes=[pltpu.VMEM((A_HEADS, 2 * tq, A_DV), BF16), pltpu.VMEM((A_HEADS, 2 * tq, 1), F32),
                        pltpu.VMEM((A_HEADS, 2 * tq, 2 * A_DV), F32)],
        compiler_params=_params(("parallel", "parallel", "arbitrary"), V7X_VMEM_LIMIT),
    )(lam_vec, norm_g.reshape(1, A_DV), qs, kr, proj)


SAMPLE_Q_ROWS = 16


def _paged_body(pt_ref, lam_ref, g_ref, q_ref, *refs, n_steps, pages_per_step, lam_init):
    g_pages = pages_per_step
    kc_refs, vc_refs = refs[0:g_pages], refs[g_pages:2 * g_pages]
    kn_ref, vn_ref, o_ref, qq_ref, m_ref, l_ref, acc_ref, bias_ref = refs[2 * g_pages:]
    p = pl.program_id(1)
    tq = SAMPLE_Q_ROWS
    hr = 2 * tq
    dv = A_DV

    @pl.when(p == 0)
    def _():
        for h in range(A_HEADS):
            qq_ref[h * hr:(h + 1) * hr] = _stack_maps(q_ref[:, h * dv:(h + 1) * dv])
        m_ref[...] = jnp.full_like(m_ref, NEG_BIG)
        l_ref[...] = jnp.zeros_like(l_ref)
        acc_ref[...] = jnp.zeros_like(acc_ref)
        r = lax.broadcasted_iota(jnp.int32, bias_ref.shape, 0) // hr
        c = lax.broadcasted_iota(jnp.int32, bias_ref.shape, 1) % A_HEADS
        bias_ref[...] = jnp.where(r == c, 0.0, NEG_BIG)

    def online_update(s, pv_of):
        m_prev = m_ref[...]
        m_new = jnp.maximum(m_prev, jnp.max(s, axis=-1, keepdims=True))
        a = jnp.exp(m_prev - m_new)
        pr = jnp.exp(s - m_new)
        l_ref[...] = a * l_ref[...] + jnp.sum(pr, axis=-1, keepdims=True)
        acc_ref[...] = a * acc_ref[...] + pv_of(pr.astype(BF16))
        m_ref[...] = m_new

    @pl.when(p < n_steps)
    def _():
        k2 = jnp.concatenate([kc_refs[g][...].astype(BF16) for g in range(g_pages)], axis=0)
        v2 = jnp.concatenate([vc_refs[g][...].astype(BF16) for g in range(g_pages)], axis=0)
        online_update(_dot_nt(qq_ref[...], k2) + bias_ref[...], lambda pb: _dot(pb, v2))

    @pl.when(p == n_steps)
    def _():
        s = jnp.concatenate([_dot_nt(qq_ref[h * hr:(h + 1) * hr], kn_ref[:, h * dv:(h + 1) * dv].astype(BF16))
                             for h in range(A_HEADS)], axis=0)
        r = lax.broadcasted_iota(jnp.int32, s.shape, 0) % tq
        c = lax.broadcasted_iota(jnp.int32, s.shape, 1)
        s = jnp.where(c <= r, s, NEG_BIG)
        online_update(s, lambda pb: jnp.concatenate(
            [_dot(pb[h * hr:(h + 1) * hr], vn_ref[:, h * dv:(h + 1) * dv].astype(BF16)) for h in range(A_HEADS)],
            axis=0))
        for h in range(A_HEADS):
            o = _diff_finalize(lam_ref, g_ref, l_ref[h * hr:(h + 1) * hr], acc_ref[h * hr:(h + 1) * hr], tq, lam_init)
            o_ref[:, h * dv:(h + 1) * dv] = o.astype(o_ref.dtype)


def paged_diff_attention(qs, kr, proj, cache_k, cache_v, page_table, layer, lam_vec, norm_g, lam_init):
    bs, n_pages = page_table.shape
    gw = GROUP_WIDTH
    tq = SAMPLE_Q_ROWS
    g_pages = 4 if n_pages % 4 == 0 else (2 if n_pages % 2 == 0 else 1)
    n_steps = n_pages // g_pages
    rows = A_HEADS * 2 * tq

    prows = PAGE_SIZE * A_HEADS
    ck = cache_k.reshape(cache_k.shape[0], DEPTH, prows, A_DV)
    cv = cache_v.reshape(cache_v.shape[0], DEPTH, prows, A_DV)

    def page(g):
        return lambda bi, p, pt: (pt[bi * n_pages + jnp.minimum(p, n_steps - 1) * g_pages + g], layer, 0, 0)

    const = lambda bi, p, pt: (0, 0)
    cache_specs = [pl.BlockSpec((None, None, prows, A_DV), page(g)) for g in range(g_pages)]
    return pl.pallas_call(
        functools.partial(_paged_body, n_steps=n_steps, pages_per_step=g_pages, lam_init=lam_init),
        name="paged_diff_attn",
        grid_spec=pltpu.PrefetchScalarGridSpec(
            num_scalar_prefetch=1, grid=(bs, n_steps + 1),
            in_specs=[pl.BlockSpec((4, A_DH), const), pl.BlockSpec((1, A_DV), const),
                      pl.BlockSpec((tq, gw), lambda bi, p, pt: (bi * (SEQ_PAD // tq), 0))]
            + cache_specs + cache_specs
            + [pl.BlockSpec((SEQ_PAD, gw), lambda bi, p, pt: (bi, 0)),
               pl.BlockSpec((SEQ_PAD, gw), lambda bi, p, pt: (bi, COL_V // gw))],
            out_specs=pl.BlockSpec((tq, gw), lambda bi, p, pt: (bi, 0)),
            scratch_shapes=[pltpu.VMEM((rows, A_DV), BF16), pltpu.VMEM((rows, 1), F32),
                            pltpu.VMEM((rows, 1), F32), pltpu.VMEM((rows, A_DV), F32),
                            pltpu.VMEM((rows, g_pages * prows), F32)]),
        out_shape=jax.ShapeDtypeStruct((bs * tq, gw), BF16),
        compiler_params=_params(("parallel", "arbitrary"), V7X_VMEM_LIMIT),
    )(page_table.reshape(-1), lam_vec, norm_g.reshape(1, A_DV), qs, *([ck] * g_pages), *([cv] * g_pages), kr, proj)


def _ssd_body(z_ref, xbc_ref, dtr_ref, cw_ref, cb_ref, dtb_ref, alog_ref, dsk_ref, ng_ref, conv0_ref, ssm0_ref,
              o_ref, ssml_ref, win_ref, st_ref, *, nc, t_real):
    c = pl.program_id(1)
    q = SEQ_PAD
    gw = GROUP_WIDTH

    @pl.when(c == 0)
    def _():
        win_ref[0:8, :] = conv0_ref[...]
        st_ref[...] = ssm0_ref[...]

    win_ref[8:8 + q, :] = xbc_ref[...]
    conv = cb_ref[...]
    for j in range(B_CONV):
        conv = conv + win_ref[pl.ds(8 - (B_CONV - 1) + j, q), :] * cw_ref[j:j + 1, :]
    win_ref[0:8, :] = win_ref[q:q + 8, :]
    act = conv * _sigmoid(conv)
    xs = act[:, 0:gw]
    bmat = act[:, gw:gw + 2 * B_STATE]
    cmat = act[:, gw + 2 * B_STATE:gw + 4 * B_STATE]

    row = lax.broadcasted_iota(jnp.int32, (q, q), 0)
    lane = lax.broadcasted_iota(jnp.int32, (q, q), 1)
    dt = _softplus(dtr_ref[...] + dtb_ref[...])
    dt = jnp.where((c * q + row < t_real) & (lane < B_HEADS), dt, 0.0)
    dta = dt * (-jnp.exp(alog_ref[...]))
    causal = row >= lane
    acum = jnp.dot(causal.astype(F32), dta, precision=HIGHEST, preferred_element_type=F32)
    acum_t = acum.T
    dt_t = dt.T
    first_half_l = lane < B_HEADDIM
    first_half_r = row < B_HEADDIM

    y_pairs = []
    for g in range(2):
        cg = cmat[:, g * B_STATE:(g + 1) * B_STATE].astype(BF16)
        bg = bmat[:, g * B_STATE:(g + 1) * B_STATE].astype(BF16)
        cb = _dot_nt(cg, bg)
        for pp in range(2):
            p = 2 * g + pp
            xs_pair = xs[:, p * LANES:(p + 1) * LANES]
            st = st_ref[p]
            y_pair = jnp.zeros((q, LANES), F32)
            for hh in range(2):
                h = 2 * p + hh
                seg = acum[:, h:h + 1] - acum_t[h:h + 1, :]
                w = cb * jnp.exp(jnp.where(causal, seg, NEG_BIG)) * dt_t[h:h + 1, :]
                xh = jnp.where(first_half_l if hh == 0 else ~first_half_l, xs_pair, 0.0)
                y_pair = y_pair + _dot(w.astype(BF16), xh.astype(BF16))
            e_in = jnp.where(first_half_l, jnp.exp(acum[:, 2 * p:2 * p + 1]), jnp.exp(acum[:, 2 * p + 1:2 * p + 2]))
            y_pair = y_pair + e_in * _dot_nt(cg, st.astype(BF16))
            y_pairs.append(y_pair)
            al0 = acum[q - 1:q, 2 * p:2 * p + 1]
            al1 = acum[q - 1:q, 2 * p + 1:2 * p + 2]
            te0 = jnp.exp(al0 - acum_t[2 * p:2 * p + 1, :]) * dt_t[2 * p:2 * p + 1, :]
            te1 = jnp.exp(al1 - acum_t[2 * p + 1:2 * p + 2, :]) * dt_t[2 * p + 1:2 * p + 2, :]
            xw = xs_pair.T * jnp.where(first_half_r, te0, te1)
            st_ref[p] = jnp.where(first_half_r, jnp.exp(al0), jnp.exp(al1)) * st + _dot(xw.astype(BF16), bg)

    y = jnp.concatenate(y_pairs, axis=1) + xs * dsk_ref[...]
    z = z_ref[...]
    gt = y * (z * _sigmoid(z))
    half = gw // 2
    outs = []
    for g in range(2):
        gg = gt[:, g * half:(g + 1) * half]
        outs.append(gg * lax.rsqrt(jnp.mean(gg * gg, axis=-1, keepdims=True) + RMS_EPS))
    o_ref[...] = (jnp.concatenate(outs, axis=1) * ng_ref[...]).astype(o_ref.dtype)

    @pl.when(c == nc - 1)
    def _():
        ssml_ref[...] = st_ref[...]


def ssd_mixer(proj, dtr, conv0, ssm0, conv_w, conv_b, dt_bias, a_log, d_skip, norm_g, b, t, t_real):
    q = SEQ_PAD
    nc = t // q
    gw = GROUP_WIDTH
    rowblk = lambda width, col: pl.BlockSpec((q, width), lambda bi, c: (bi * nc + c, col))
    vec = lambda width: pl.BlockSpec((1, width), lambda bi, c: (0, 0))
    pad8 = lambda a: jnp.zeros((LANES,), F32).at[:B_HEADS].set(a).reshape(1, LANES)
    conv0p = jnp.pad(conv0, ((0, 0), (8 - (B_CONV - 1), 0), (0, 0)))
    ssm0p = ssm0.reshape(b, B_HEADS // 2, 2 * B_HEADDIM, B_STATE)
    o, ssml = pl.pallas_call(
        functools.partial(_ssd_body, nc=nc, t_real=t_real), name="ssd", grid=(b, nc),
        in_specs=[rowblk(gw, COL_Z // gw), rowblk(B_CONV_CH, COL_XBC // B_CONV_CH), rowblk(LANES, 0),
                  pl.BlockSpec((B_CONV, B_CONV_CH), lambda bi, c: (0, 0)), vec(B_CONV_CH), vec(LANES), vec(LANES),
                  vec(gw), vec(gw),
                  pl.BlockSpec((None, 8, B_CONV_CH), lambda bi, c: (bi, 0, 0)),
                  pl.BlockSpec((None, B_HEADS // 2, 2 * B_HEADDIM, B_STATE), lambda bi, c: (bi, 0, 0, 0))],
        out_specs=[pl.BlockSpec((q, gw), lambda bi, c: (bi * nc + c, 0)),
                   pl.BlockSpec((None, B_HEADS // 2, 2 * B_HEADDIM, B_STATE), lambda bi, c: (bi, 0, 0, 0))],
        out_shape=[jax.ShapeDtypeStruct((b * t, gw), BF16),
                   jax.ShapeDtypeStruct((b, B_HEADS // 2, 2 * B_HEADDIM, B_STATE), F32)],
        scratch_shapes=[pltpu.VMEM((8 + q, B_CONV_CH), F32),
                        pltpu.VMEM((B_HEADS // 2, 2 * B_HEADDIM, B_STATE), F32)],
        compiler_params=_params(("parallel", "arbitrary")),
    )(proj, proj, dtr, conv_w, conv_b.reshape(1, -1), pad8(dt_bias), pad8(a_log),
      jnp.repeat(d_skip, B_HEADDIM).reshape(1, gw), norm_g.reshape(1, gw), conv0p, ssm0p)
    return o, ssml.reshape(b, B_HEADS, B_HEADDIM, B_STATE)


def _rwkv_prep_body(r_ref, k_ref, v_ref, u_ref, sh_ref, vf_ref, mu_ref, w0_ref, w1_ref, w2_ref, a0_ref, a1_ref, a2_ref,
                    g1_ref, g2_ref, kk_ref, ka_ref, rk_ref, vmu_ref, v0_ref, v1_ref, v2_ref, seg_ref,
                    nkk_o, dr_o, d_o, b_o, km_o, v_o, br_o, kr_o, g_o, bon_o, carry_ref, *, first_layer):
    c = pl.program_id(1)
    gw = GROUP_WIDTH

    @pl.when(c == 0)
    def _():
        carry_ref[...] = sh_ref[...]

    row0 = lax.broadcasted_iota(jnp.int32, r_ref.shape, 0) == 0

    def delta(ref, i):
        x = ref[...]
        prev = jnp.where(row0, carry_ref[:, i * gw:(i + 1) * gw], pltpu.roll(x, 1, 0))
        carry_ref[:, i * gw:(i + 1) * gw] = x[x.shape[0] - 1:x.shape[0]]
        return x, prev - x

    r_in, d_r = delta(r_ref, 0)
    k_in, d_k = delta(k_ref, 1)
    v_in, d_v = delta(v_ref, 2)
    u_in, d_u = delta(u_ref, 3)
    mu = mu_ref[...]
    r = r_in + d_r * mu[0:1]
    k = k_in + d_k * mu[1:2]
    v = v_in + d_v * mu[2:3]
    xw = u_in + d_u * mu[3:4]
    xa = u_in + d_u * mu[4:5]
    xg = u_in + d_u * mu[5:6]

    def lora(x, a_ref, b_ref, act):
        return _dot(act(_dot(x.astype(BF16), a_ref[...])).astype(BF16), b_ref[...])

    w = -_softplus(-(w0_ref[...] + lora(xw, w1_ref, w2_ref, jnp.tanh))) - 0.5
    a = _sigmoid(a0_ref[...] + lora(xa, a1_ref, a2_ref, lambda t: t))
    g = lora(xg, g1_ref, g2_ref, _sigmoid)
    if not first_layer:
        xv = u_in + d_u * vmu_ref[...]
        v = v + (vf_ref[...] - v) * _sigmoid(v0_ref[...] + lora(xv, v1_ref, v2_ref, lambda t: t))
    seg = seg_ref[...]
    kk = k * kk_ref[...]
    kk = kk / jnp.maximum(jnp.sqrt(_seg_dot(kk * kk, seg)), 1e-12)
    km = k * (1.0 + (a - 1.0) * ka_ref[...])
    d = jnp.exp(-jnp.exp(w))
    bvec = kk * a
    nkk_o[...] = -kk
    dr_o[...] = d * r
    d_o[...] = d
    b_o[...] = bvec
    km_o[...] = km
    v_o[...] = v
    br_o[...] = _seg_dot(bvec * r, seg)
    kr_o[...] = _seg_dot(km * r, seg)
    g_o[...] = g
    bon_o[...] = _seg_dot(r * km * rk_ref[...], seg) * v


def _pad_cols(w, n):
    return jnp.pad(w, ((0, 0), (0, n - w.shape[1])))


def _pad_rows(w, n):
    return jnp.pad(w, ((0, n - w.shape[0]), (0, 0)))


def _head_seg_matrix(scale):
    h = jnp.arange(GROUP_WIDTH) // C_HEAD
    return ((h[:, None] == h[None, :]).astype(F32) * scale).astype(BF16)


def rwkv_prep(proj, shift0, v_first, lp, b, t, first_layer):
    q = SEQ_PAD
    nc = t // q
    gw = GROUP_WIDTH
    rowblk = lambda col: pl.BlockSpec((q, gw), lambda bi, c: (bi * nc + c, col))
    full = lambda a: pl.BlockSpec(a.shape, lambda bi, c: (0,) * a.ndim)
    row1 = lambda a: a.reshape(1, gw)
    lo = lambda a, bmat: (_pad_cols(a, LANES).astype(BF16), _pad_rows(bmat, LANES).astype(BF16))
    w1, w2 = lo(lp['w1'], lp['w2'])
    a1, a2 = lo(lp['a1'], lp['a2'])
    g1, g2 = lo(lp['g1'], lp['g2'])
    v1, v2 = lo(lp['v1'], lp['v2'])
    params = [lp['mu'], row1(lp['w0']), w1, w2, row1(lp['a0']), a1, a2, g1, g2, row1(lp['kk']), row1(lp['ka']),
              row1(lp['rk']), row1(lp['vmu']), row1(lp['v0']), v1, v2, _head_seg_matrix(1.0)]
    outs = pl.pallas_call(
        functools.partial(_rwkv_prep_body, first_layer=first_layer), name="rwkv_prep", grid=(b, nc),
        in_specs=[rowblk(COL_RKVU // gw), rowblk(COL_RKVU // gw + 1), rowblk(COL_RKVU // gw + 2),
                  rowblk(COL_RKVU // gw + 3), pl.BlockSpec((None, 1, 4 * gw), lambda bi, c: (bi, 0, 0)),
                  rowblk(0)] + [full(a) for a in params],
        out_specs=[rowblk(0)] * 10,
        out_shape=[jax.ShapeDtypeStruct((b * t, gw), F32)] * 10,
        scratch_shapes=[pltpu.VMEM((1, 4 * gw), F32)],
        compiler_params=_params(("parallel", "arbitrary")),
    )(proj, proj, proj, proj, shift0.reshape(b, 1, 4 * gw), v_first, *params)
    return outs


def _rwkv_scan_body(nkk_ref, dr_ref, d_ref, b_ref, km_ref, v_ref, br_ref, kr_ref, s0_ref, o_ref, sl_ref, st_ref,
                    cb_ref, *, nc, tc, t_real):
    c = pl.program_id(1)
    n = C_HEAD
    npairs = C_HEADS // 2
    col_refs = (nkk_ref, dr_ref, d_ref, b_ref, km_ref)

    @pl.when(c == 0)
    def _():
        st_ref[...] = s0_ref[...]

    @pl.when(c * tc >= t_real)
    def _():
        o_ref[...] = jnp.zeros_like(o_ref)

    @pl.when(c * tc < t_real)
    def _():
        key = lax.broadcasted_iota(jnp.int32, (tc, n, LANES), 1)
        lane = lax.broadcasted_iota(jnp.int32, (tc, n, LANES), 2)
        diag = (lane == key) | (lane == key + n)
        r_h = lax.broadcasted_iota(jnp.int32, (2 * LANES, LANES), 0) % LANES // n
        c_h = lax.broadcasted_iota(jnp.int32, (2 * LANES, LANES), 1) // n
        spread = (r_h == c_h).astype(BF16)

        def stage_columns(p):
            for x, ref in enumerate(col_refs):
                v0 = ref[:, p * LANES:(p + 1) * LANES]
                hi = v0.astype(BF16)
                lo = (v0 - hi.astype(F32)).astype(BF16)
                pieces = [jnp.where(diag, jnp.broadcast_to(pc.astype(F32)[:, None, :], (tc, n, LANES)), 0.0)
                          .astype(BF16).reshape(tc * n, LANES) for pc in (hi, lo)]
                cb = _dot(jnp.concatenate(pieces, axis=1), spread)
                cb_ref[x, p] = cb.reshape(tc, n, LANES)

        def recur(p):
            st = st_ref[p]
            sl = slice(p * LANES, (p + 1) * LANES)
            for tt in range(tc):
                sa = jnp.sum(st * cb_ref[0, p, tt], axis=0, keepdims=True)
                u = jnp.sum(st * cb_ref[1, p, tt], axis=0, keepdims=True)
                vrow = v_ref[tt:tt + 1, sl]
                o_ref[tt:tt + 1, sl] = u + sa * br_ref[tt:tt + 1, sl] + vrow * kr_ref[tt:tt + 1, sl]
                st = st * cb_ref[2, p, tt] + cb_ref[3, p, tt] * sa + cb_ref[4, p, tt] * vrow
            st_ref[p] = st

        stage_columns(0)
        for p in range(npairs):
            if p + 1 < npairs:
                stage_columns(p + 1)
            recur(p)

    @pl.when(c == nc - 1)
    def _():
        sl_ref[...] = st_ref[...]


def rwkv_scan(nkk, dr, d, bv, km, v, br, kr, s0, b, t, t_real):
    tc = 16 if t_real % 16 == 0 else 8
    assert t_real % tc == 0 and t % tc == 0
    nc = t // tc
    gw = GROUP_WIDTH
    n = C_HEAD
    rowblk = pl.BlockSpec((tc, gw), lambda bi, c: (bi * nc + c, 0))
    sblk = pl.BlockSpec((None, C_HEADS // 2, n, 2 * n), lambda bi, c: (bi, 0, 0, 0))
    s0t = s0.reshape(b, C_HEADS // 2, 2, n, n).transpose(0, 1, 4, 2, 3).reshape(b, C_HEADS // 2, n, 2 * n)
    o, sl = pl.pallas_call(
        functools.partial(_rwkv_scan_body, nc=nc, tc=tc, t_real=t_real), name="rwkv_scan", grid=(b, nc),
        in_specs=[rowblk] * 8 + [sblk],
        out_specs=[rowblk, sblk],
        out_shape=[jax.ShapeDtypeStruct((b * t, gw), F32), jax.ShapeDtypeStruct((b, C_HEADS // 2, n, 2 * n), F32)],
        scratch_shapes=[pltpu.VMEM((C_HEADS // 2, n, 2 * n), F32),
                        pltpu.VMEM((5, C_HEADS // 2, tc, n, 2 * n), F32)],
        compiler_params=_params(("parallel", "arbitrary"), V7X_VMEM_LIMIT),
    )(nkk, dr, d, bv, km, v, br, kr, s0t)
    s_last = sl.reshape(b, C_HEADS // 2, n, 2, n).transpose(0, 1, 3, 4, 2).reshape(b, C_HEADS, n, n)
    return o, s_last


def _rwkv_post_body(o_ref, g_ref, bon_ref, lg_ref, lb_ref, seg_ref, out_ref):
    o = o_ref[...]
    seg = seg_ref[...]
    xc = o - _seg_dot(o, seg)
    var = _seg_dot(xc * xc, seg)
    y = xc * lax.rsqrt(var + C_LN_EPS) * lg_ref[...] + lb_ref[...]
    out_ref[...] = ((y + bon_ref[...]) * g_ref[...]).astype(out_ref.dtype)


def rwkv_post(o, g, bonus, ln_g, ln_b):
    rows, gw = o.shape
    q = _tile(rows, 512, 16)
    rowblk = pl.BlockSpec((q, gw), lambda i: (i, 0))
    vec = pl.BlockSpec((1, gw), lambda i: (0, 0))
    return pl.pallas_call(
        _rwkv_post_body, name="rwkv_post", grid=(rows // q,),
        in_specs=[rowblk, rowblk, rowblk, vec, vec, pl.BlockSpec((gw, gw), lambda i: (0, 0))],
        out_specs=rowblk, out_shape=jax.ShapeDtypeStruct((rows, gw), BF16),
        compiler_params=_params(("parallel",)),
    )(o, g, bonus, ln_g.reshape(1, gw), ln_b.reshape(1, gw), _head_seg_matrix(1.0 / C_HEAD))


S5_LANES = D_GROUPS * D_STATE
S5_GB = GROUP_WIDTH // LANES


def _s5_body(u_ref, bre_ref, bim_ref, cre_ref, cim_ref, are_ref, aim_ref, pre_ref, pim_ref, dsk_ref, gw_ref, gb_ref,
             h0re_ref, h0im_ref, o_ref, hlre_ref, hlim_ref, hre_ref, him_ref, cr_ref, ci_ref, *, nc, t_real):
    c = pl.program_id(1)
    q = SEQ_PAD
    sb = S5_LANES // S5_GB

    @pl.when(c == 0)
    def _():
        cr_ref[...] = h0re_ref[...]
        ci_ref[...] = h0im_ref[...]

    u = u_ref[...]
    ub = u.astype(BF16)
    for gb in range(S5_GB):
        ug = ub[:, gb * LANES:(gb + 1) * LANES]
        hre_ref[:, gb * sb:(gb + 1) * sb] = _dot(ug, bre_ref[gb])
        him_ref[:, gb * sb:(gb + 1) * sb] = _dot(ug, bim_ref[gb])

    row8 = lax.broadcasted_iota(jnp.int32, (8, S5_LANES), 0)
    pre, pim = pre_ref[...], pim_ref[...]

    def group(i, carry):
        cr, ci = carry
        off = pl.multiple_of(i * 8, 8)
        xr = hre_ref[pl.ds(off, 8), :]
        xi = him_ref[pl.ds(off, 8), :]
        for k, sh in enumerate((1, 2, 4)):
            ar, ai = are_ref[k:k + 1, :], aim_ref[k:k + 1, :]
            sr = jnp.where(row8 >= sh, pltpu.roll(xr, sh, 0), 0.0)
            si = jnp.where(row8 >= sh, pltpu.roll(xi, sh, 0), 0.0)
            xr, xi = xr + ar * sr - ai * si, xi + ar * si + ai * sr
        xr, xi = xr + pre * cr - pim * ci, xi + pre * ci + pim * cr
        hre_ref[pl.ds(off, 8), :] = xr
        him_ref[pl.ds(off, 8), :] = xi
        return xr[7:8], xi[7:8]

    cr, ci = lax.fori_loop(0, q // 8, group, (cr_ref[...], ci_ref[...]))
    cr_ref[...] = cr
    ci_ref[...] = ci

    @pl.when(c == (t_real - 1) // q)
    def _():
        rl = (t_real - 1) % q
        hlre_ref[...] = hre_ref[rl:rl + 1, :]
        hlim_ref[...] = him_ref[rl:rl + 1, :]

    ys = []
    for gb in range(S5_GB):
        hr = hre_ref[:, gb * sb:(gb + 1) * sb].astype(BF16)
        hi = him_ref[:, gb * sb:(gb + 1) * sb].astype(BF16)
        ys.append(_dot(hr, cre_ref[gb]) - _dot(hi, cim_ref[gb]))
    y = jnp.concatenate(ys, axis=1) + dsk_ref[...] * u
    y = 0.5 * y * (1.0 + jnp.tanh(math.sqrt(2.0 / math.pi) * (y + 0.044715 * (y * y * y))))
    gate = _sigmoid(_dot(y.astype(BF16), gw_ref[...]) + gb_ref[...])
    o_ref[...] = (y * gate).astype(o_ref.dtype)


def s5_params(a_re, a_im, log_dt, b_re, b_im, c_re, c_im):
    step = jnp.exp(log_dt)[:, None]
    zr, zi = a_re * step, a_im * step

    def apow(k):
        e = jnp.exp(k * zr)
        return (e * jnp.cos(k * zi)).reshape(-1), (e * jnp.sin(k * zi)).reshape(-1)

    abr, abi = jnp.exp(zr) * jnp.cos(zi), jnp.exp(zr) * jnp.sin(zi)
    den = a_re * a_re + a_im * a_im
    fr = ((abr - 1.0) * a_re + abi * a_im) / den
    fi = (abi * a_re - (abr - 1.0) * a_im) / den
    bbr = fr[..., None] * b_re - fi[..., None] * b_im
    bbi = fr[..., None] * b_im + fi[..., None] * b_re
    gpb = D_GROUPS // S5_GB
    eye = jnp.eye(gpb, dtype=F32)

    def in_blocks(x):
        x = x.reshape(S5_GB, gpb, D_STATE, D_GROUP_CH)
        return jnp.einsum('bgnc,gh->bgchn', x, eye).reshape(S5_GB, gpb * D_GROUP_CH, gpb * D_STATE).astype(BF16)

    def out_blocks(x):
        x = x.reshape(S5_GB, gpb, D_GROUP_CH, D_STATE)
        return jnp.einsum('bgcn,gh->bgnhc', x, eye).reshape(S5_GB, gpb * D_STATE, gpb * D_GROUP_CH).astype(BF16)

    pows = [apow(float(k)) for k in (1, 2, 4)]
    p8 = [apow(float(k)) for k in range(1, 9)]
    return dict(bre=in_blocks(bbr), bim=in_blocks(bbi), cre=out_blocks(c_re), cim=out_blocks(c_im),
                are=jnp.stack([p[0] for p in pows]), aim=jnp.stack([p[1] for p in pows]),
                pre=jnp.stack([p[0] for p in p8]), pim=jnp.stack([p[1] for p in p8]))


def s5_mixer(proj, sp, d_skip, glu_w, glu_b, h0re, h0im, b, t, t_real):
    q = SEQ_PAD
    nc = t // q
    gw = GROUP_WIDTH
    full = lambda a: pl.BlockSpec(a.shape, lambda bi, c: (0,) * a.ndim)
    st = pl.BlockSpec((None, 1, S5_LANES), lambda bi, c: (bi, 0, 0))
    consts = [sp['bre'], sp['bim'], sp['cre'], sp['cim'], sp['are'], sp['aim'], sp['pre'], sp['pim'],
              d_skip.reshape(1, gw), glu_w.astype(BF16), glu_b.reshape(1, gw)]
    o, hlre, hlim = pl.pallas_call(
        functools.partial(_s5_body, nc=nc, t_real=t_real), name="s5", grid=(b, nc),
        in_specs=[pl.BlockSpec((q, gw), lambda bi, c: (bi * nc + c, COL_S5 // gw))] + [full(a) for a in consts]
        + [st, st],
        out_specs=[pl.BlockSpec((q, gw), lambda bi, c: (bi * nc + c, 0)), st, st],
        out_shape=[jax.ShapeDtypeStruct((b * t, gw), BF16), jax.ShapeDtypeStruct((b, 1, S5_LANES), F32),
                   jax.ShapeDtypeStruct((b, 1, S5_LANES), F32)],
        scratch_shapes=[pltpu.VMEM((q, S5_LANES), F32), pltpu.VMEM((q, S5_LANES), F32),
                        pltpu.VMEM((1, S5_LANES), F32), pltpu.VMEM((1, S5_LANES), F32)],
        compiler_params=_params(("parallel", "arbitrary")),
    )(proj, *consts, h0re.reshape(b, 1, S5_LANES), h0im.reshape(b, 1, S5_LANES))
    return o, hlre.reshape(b, D_GROUPS, D_STATE), hlim.reshape(b, D_GROUPS, D_STATE)


def _pad_seq(a, b, t, t_pad):
    return jnp.pad(a.reshape(b, t, -1), ((0, 0), (0, t_pad - t), (0, 0))).reshape(b * t_pad, -1)


def kernel(x_prompt, x_sample, cache_k, cache_v, state_ssm, state_conv, state_wkv, state_shift, state_s5_re, state_s5_im, page_table, meta_tokens, ln_in_g, ln_in_b, w_in, w_out, ln1_g, ln1_b, ln2_g, ln2_b, a_lambda, a_norm_g, b_conv_w, b_conv_b, b_dt_bias, b_a_log, b_d, b_norm_g, c_mu, c_w0, c_w1, c_w2, c_a0, c_a1, c_a2, c_v_mu, c_v0, c_v1, c_v2, c_g1, c_g2, c_kk, c_ka, c_rk, c_ln_g, c_ln_b, d_a_re, d_a_im, d_log_dt, d_b_re, d_b_im, d_c_re, d_c_im, d_d, d_glu_w, d_glu_b, f_w1, f_w3, f_w2, m_router, m_w1, m_w3, m_w2):
    bp, seq, d = x_prompt.shape
    bs, ts, _ = x_sample.shape
    tp = seq + N_META
    tpp = -(-tp // SEQ_PAD) * SEQ_PAD
    tsp = SEQ_PAD
    rp = bp * tpp
    past_len = page_table.shape[1] * PAGE_SIZE
    gw = GROUP_WIDTH

    meta = jnp.broadcast_to(meta_tokens[None], (bp, N_META, d))
    xp = jnp.pad(jnp.concatenate([meta, x_prompt], axis=1), ((0, 0), (0, tpp - tp), (0, 0)))
    x0 = jnp.concatenate([xp.reshape(rp, d), x_sample.reshape(bs * ts, d)], axis=0)
    x, xb = ln_in(x0, ln_in_g, ln_in_b)

    dt0 = COL_RKVU
    w_main = jnp.concatenate([w_in[:, :, :dt0], w_in[:, :, dt0 + B_HEADS:]], axis=2)
    w_dt = jnp.pad(w_in[:, :, dt0:dt0 + B_HEADS], ((0, 0), (0, 0), (0, LANES - B_HEADS)))
    router_w = jnp.pad(m_router, ((0, 0), (0, 0), (0, LANES - N_EXPERTS)))

    tabs_p = _rope_tables(jnp.arange(tpp, dtype=jnp.int32))
    tabs_s = _rope_tables(past_len + jnp.arange(tsp, dtype=jnp.int32))

    zeros = lambda *s: jnp.zeros(s, F32)
    groups = [
        dict(b=bp, t=tpp, tr=tp, conv=zeros(bp, DEPTH, B_CONV - 1, B_CONV_CH),
             ssm=zeros(bp, DEPTH, B_HEADS, B_HEADDIM, B_STATE), shift=zeros(bp, DEPTH, 4 * gw),
             wkv=zeros(bp, DEPTH, C_HEADS, C_HEAD, C_HEAD), s5re=zeros(bp, DEPTH, D_GROUPS, D_STATE),
             s5im=zeros(bp, DEPTH, D_GROUPS, D_STATE), tabs=tabs_p),
        dict(b=bs, t=tsp, tr=ts, conv=state_conv, ssm=state_ssm, shift=state_shift, wkv=state_wkv,
             s5re=state_s5_re, s5im=state_s5_im, tabs=tabs_s),
    ]
    outs = [dict(k=[], v=[], ssm=[], conv=[], wkv=[], shift=[], s5re=[], s5im=[]) for _ in groups]
    v_first = [None, None]

    for l in range(DEPTH):
        lam_init = 0.8 - 0.6 * math.exp(-0.3 * l)
        proj_all = matmul_layer(xb, w_main, l, 512, "proj_in")
        dtr_all = matmul_layer(xb, w_dt, l, LANES, "proj_dt")
        sp = s5_params(d_a_re[l], d_a_im[l], d_log_dt[l], d_b_re[l], d_b_im[l], d_c_re[l], d_c_im[l])
        lv = max(l - 1, 0)
        lp = dict(mu=c_mu[l], w0=c_w0[l], w1=c_w1[l], w2=c_w2[l], a0=c_a0[l], a1=c_a1[l], a2=c_a2[l], g1=c_g1[l],
                  g2=c_g2[l], kk=c_kk[l], ka=c_ka[l], rk=c_rk[l].reshape(-1), vmu=c_v_mu[lv], v0=c_v0[lv],
                  v1=c_v1[lv], v2=c_v2[lv])
        mixes = []
        for gi, (gr, out) in enumerate(zip(groups, outs)):
            b, t, tr = gr['b'], gr['t'], gr['tr']
            if gi == 0:
                proj, dtr = proj_all, dtr_all
            else:
                proj = _pad_seq(proj_all[rp:], b, tr, t)
                dtr = _pad_seq(dtr_all[rp:], b, tr, t)
            qs, kr = rope_qk(proj, gr['tabs'], b, t)
            if gi == 0:
                o_a = flash_diff_attention(qs, kr, proj, a_lambda[l], a_norm_g[l], b, t, lam_init)
            else:
                o_a = paged_diff_attention(qs, kr, proj, cache_k, cache_v, page_table, l, a_lambda[l], a_norm_g[l],
                                           lam_init)
                o_a = _pad_seq(o_a.reshape(b, SAMPLE_Q_ROWS, gw)[:, :tr], b, tr, t)
            o_b, ssm_new = ssd_mixer(proj, dtr, gr['conv'][:, l], gr['ssm'][:, l], b_conv_w[l], b_conv_b[l],
                                     b_dt_bias[l], b_a_log[l], b_d[l], b_norm_g[l], b, t, tr)
            vf = proj[:, :gw] if l == 0 else v_first[gi]
            nkk, dr, dd, bv, km, v, br, krr, g, bonus = rwkv_prep(proj, gr['shift'][:, l], vf, lp, b, t, l == 0)
            if l == 0:
                v_first[gi] = v
            o_scan, wkv_new = rwkv_scan(nkk, dr, dd, bv, km, v, br, krr, gr['wkv'][:, l], b, t, tr)
            o_c = rwkv_post(o_scan, g, bonus, c_ln_g[l], c_ln_b[l])
            o_d, s5re_new, s5im_new = s5_mixer(proj, sp, d_d[l], d_glu_w[l], d_glu_b[l], gr['s5re'][:, l],
                                               gr['s5im'][:, l], b, t, tr)
            mix = jnp.concatenate([o_a, o_b, o_c, o_d], axis=1)
            mixes.append(mix if gi == 0 else mix.reshape(b, t, 4 * gw)[:, :tr].reshape(b * tr, 4 * gw))
            p3 = proj[:b * t].reshape(b, t, -1)
            out['k'].append(kr.reshape(b, t, A_HEADS, A_DV)[:, :tr])
            out['v'].append(p3[:, :tr, COL_V:COL_V + gw].reshape(b, tr, A_HEADS, A_DV))
            out['ssm'].append(ssm_new)
            out['conv'].append(p3[:, tr - (B_CONV - 1):tr, COL_XBC:COL_XBC + B_CONV_CH])
            out['wkv'].append(wkv_new)
            out['shift'].append(p3[:, tr - 1, COL_RKVU:COL_RKVU + 4 * gw])
            out['s5re'].append(s5re_new)
            out['s5im'].append(s5im_new)
        mix_all = jnp.concatenate(mixes, axis=0)
        x, xb = matmul_resid_ln(mix_all, w_out, l, x, ln1_g[l], ln1_b[l], "proj_out")
        if l % 2 == 0:
            hb = swiglu_up(xb, f_w1, f_w3, l // 2)
            x, xb = matmul_resid_ln(hb, f_w2, l // 2, x, ln2_g[l], ln2_b[l], "ffn_down")
        else:
            x, xb = moe_layer(x, xb, router_w, m_w1, m_w3, m_w2, l // 2, ln2_g[l], ln2_b[l])

    y_p = x[:rp].reshape(bp, tpp, d)[:, N_META:tp]
    y_s = x[rp:].reshape(bs, ts, d)
    st = lambda gi, key: jnp.stack(outs[gi][key], axis=1)
    return (y_p, y_s, st(0, 'k'), st(0, 'v'), st(1, 'k'), st(1, 'v'), st(0, 'ssm'), st(1, 'ssm'),
            st(0, 'conv'), st(1, 'conv'), st(0, 'wkv'), st(1, 'wkv'), st(0, 'shift'), st(1, 'shift'),
            st(0, 's5re'), st(0, 's5im'), st(1, 's5re'), st(1, 's5im'))
```

```python
import functools
import math

import jax
import jax.numpy as jnp
from jax import lax
from jax.experimental import pallas as pl
from jax.experimental.pallas import tpu as pltpu

F32 = jnp.float32
BF16 = jnp.bfloat16
HIGHEST = lax.Precision.HIGHEST

D_MODEL = 2048
DEPTH = 4
PAGE_SIZE = 128
N_META = 16
GROUP_WIDTH = 512
A_HEADS = 4
A_DH = 64
A_DV = 128
ROT_DIM = 16
ROPE_THETA = 500000.0
B_HEADS = 8
B_HEADDIM = 64
B_STATE = 128
B_CONV = 4
B_CONV_CH = 1024
C_HEADS = 8
C_HEAD = 64
C_LN_EPS = 64e-5
D_GROUPS = 32
D_GROUP_CH = 16
D_STATE = 64
FF_DIM = 5632
N_EXPERTS = 8
TOP_K = 2
ALPHA = (2.0 * DEPTH) ** 0.25
LN_EPS = 1e-5
RMS_EPS = 1e-5
NEG_BIG = -1e30

SEQ_PAD = 128
LANES = 128
V7X_VMEM_LIMIT = 56 * 1024 * 1024
MOE_BLOCK_ROWS = 512
N_PROJ = 11 * GROUP_WIDTH
COL_Q, COL_K, COL_V, COL_Z, COL_XBC, COL_RKVU, COL_S5 = 0, 512, 1024, 1536, 2048, 3072, 5120


def _tile(n, target, mult):
    best = None
    for d in range(mult, min(n, target) + 1, mult):
        if n % d == 0:
            best = d
    return n if best is None else best


def _params(sem, vmem=None):
    return pltpu.CompilerParams(dimension_semantics=sem, vmem_limit_bytes=vmem)


def _sigmoid(x):
    return 1.0 / (1.0 + jnp.exp(-x))


def _softplus(x):
    return jnp.maximum(x, 0.0) + jnp.log(1.0 + jnp.exp(-jnp.abs(x)))


def _ln_rows(x, g, b):
    mu = jnp.mean(x, axis=-1, keepdims=True)
    xc = x - mu
    var = jnp.mean(xc * xc, axis=-1, keepdims=True)
    return xc * lax.rsqrt(var + LN_EPS) * g + b


def _dot(a, b):
    return jnp.dot(a, b, preferred_element_type=F32)


def _dot_nt(a, b):
    return lax.dot_general(a, b, (((1,), (1,)), ((), ())), preferred_element_type=F32)


def _seg_dot(x, p):
    hi = x.astype(BF16)
    lo = (x - hi.astype(F32)).astype(BF16)
    return _dot(hi, p) + _dot(lo, p)


def _ln_in_body(x_ref, g_ref, b_ref, o_ref, ob_ref):
    y = _ln_rows(x_ref[...], g_ref[...], b_ref[...])
    o_ref[...] = y
    ob_ref[...] = y.astype(BF16)


def ln_in(x, g, b):
    m, d = x.shape
    tm = _tile(m, 512, 16)
    return pl.pallas_call(
        _ln_in_body, name="ln_in", grid=(m // tm,),
        in_specs=[pl.BlockSpec((tm, d), lambda i: (i, 0)), pl.BlockSpec((1, d), lambda i: (0, 0)),
                  pl.BlockSpec((1, d), lambda i: (0, 0))],
        out_specs=[pl.BlockSpec((tm, d), lambda i: (i, 0)), pl.BlockSpec((tm, d), lambda i: (i, 0))],
        out_shape=[jax.ShapeDtypeStruct((m, d), F32), jax.ShapeDtypeStruct((m, d), BF16)],
        compiler_params=_params(("parallel",)),
    )(x, g.reshape(1, d), b.reshape(1, d))


def _mm_body(x_ref, w_ref, o_ref):
    o_ref[...] = _dot(x_ref[...], w_ref[...].astype(BF16))


def matmul_layer(xb, w, layer, tn, name):
    m, k = xb.shape
    n = w.shape[2]
    tm = _tile(m, 1280, 16)
    return pl.pallas_call(
        _mm_body, name=name, grid=(m // tm, n // tn),
        in_specs=[pl.BlockSpec((tm, k), lambda i, j: (i, 0)),
                  pl.BlockSpec((None, k, tn), lambda i, j: (layer, 0, j))],
        out_specs=pl.BlockSpec((tm, tn), lambda i, j: (i, j)),
        out_shape=jax.ShapeDtypeStruct((m, n), F32),
        compiler_params=_params(("parallel", "arbitrary"), V7X_VMEM_LIMIT),
    )(xb, w)


def _router_body(x_ref, w_ref, o_ref):
    o_ref[...] = jnp.dot(x_ref[...], w_ref[...], precision=HIGHEST, preferred_element_type=F32)


def router_logits(x, w, layer):
    m, k = x.shape
    n = w.shape[2]
    tm = _tile(m, 512, 8)
    return pl.pallas_call(
        _router_body, name="router", grid=(m // tm,),
        in_specs=[pl.BlockSpec((tm, k), lambda i: (i, 0)), pl.BlockSpec((None, k, n), lambda i: (layer, 0, 0))],
        out_specs=pl.BlockSpec((tm, n), lambda i: (i, 0)),
        out_shape=jax.ShapeDtypeStruct((m, n), F32),
        compiler_params=_params(("parallel",), V7X_VMEM_LIMIT),
    )(x, w)


def _glu_body(x_ref, w1_ref, w3_ref, o_ref):
    x = x_ref[...]
    a = _dot(x, w1_ref[...].astype(BF16))
    b = _dot(x, w3_ref[...].astype(BF16))
    o_ref[...] = (a * _sigmoid(a) * b).astype(BF16)


def swiglu_up(xb, w1, w3, layer):
    m, k = xb.shape
    n = w1.shape[2]
    tm = _tile(m, 1280, 16)
    tn = 512
    wspec = pl.BlockSpec((None, k, tn), lambda i, j: (layer, 0, j))
    return pl.pallas_call(
        _glu_body, name="swiglu_up", grid=(m // tm, n // tn),
        in_specs=[pl.BlockSpec((tm, k), lambda i, j: (i, 0)), wspec, wspec],
        out_specs=pl.BlockSpec((tm, tn), lambda i, j: (i, j)),
        out_shape=jax.ShapeDtypeStruct((m, n), BF16),
        compiler_params=_params(("parallel", "arbitrary"), V7X_VMEM_LIMIT),
    )(xb, w1, w3)


def _mm_ln_body(h_ref, w_ref, r_ref, g_ref, b_ref, o_ref, ob_ref, acc_ref, *, nk):
    k = pl.program_id(1)

    @pl.when(k == 0)
    def _():
        acc_ref[...] = jnp.zeros_like(acc_ref)

    acc_ref[...] += _dot(h_ref[...], w_ref[...].astype(BF16))

    @pl.when(k == nk - 1)
    def _():
        y = _ln_rows(ALPHA * r_ref[...] + acc_ref[...], g_ref[...], b_ref[...])
        o_ref[...] = y
        ob_ref[...] = y.astype(BF16)


def matmul_resid_ln(hb, w, layer, resid, g, b, name):
    m, k = hb.shape
    n = w.shape[2]
    tm = _tile(m, 640, 16)
    tk = 512
    nk = k // tk
    row = pl.BlockSpec((tm, n), lambda i, kk: (i, 0))
    vec = pl.BlockSpec((1, n), lambda i, kk: (0, 0))
    return pl.pallas_call(
        functools.partial(_mm_ln_body, nk=nk), name=name, grid=(m // tm, nk),
        in_specs=[pl.BlockSpec((tm, tk), lambda i, kk: (i, kk)),
                  pl.BlockSpec((None, tk, n), lambda i, kk: (layer, kk, 0)), row, vec, vec],
        out_specs=[row, row],
        out_shape=[jax.ShapeDtypeStruct((m, n), F32), jax.ShapeDtypeStruct((m, n), BF16)],
        scratch_shapes=[pltpu.VMEM((tm, n), F32)],
        compiler_params=_params(("parallel", "arbitrary"), V7X_VMEM_LIMIT),
    )(hb, w, resid, g.reshape(1, n), b.reshape(1, n))


def _gather_body(idx_ref, src_ref, o_ref, buf, sem, *, br):
    base = pl.program_id(0) * br

    def row_copy(r, src_row):
        return pltpu.make_async_copy(src_ref.at[pl.ds(src_row, 1)], buf.at[pl.ds(r, 1)], sem)

    def start(r, c):
        row_copy(r, idx_ref[base + r]).start()
        return c

    def wait(r, c):
        row_copy(r, 0).wait()
        return c

    lax.fori_loop(0, br, start, 0)
    lax.fori_loop(0, br, wait, 0)
    o_ref[...] = buf[...].astype(BF16)


def gather_rows_bf16(src, idx):
    r = idx.shape[0]
    d = src.shape[1]
    br = _tile(r, 256, 16)
    return pl.pallas_call(
        functools.partial(_gather_body, br=br), name="moe_gather",
        grid_spec=pltpu.PrefetchScalarGridSpec(
            num_scalar_prefetch=1, grid=(r // br,),
            in_specs=[pl.BlockSpec(memory_space=pl.ANY)],
            out_specs=pl.BlockSpec((br, d), lambda i, idx_ref: (i, 0)),
            scratch_shapes=[pltpu.VMEM((br, d), F32), pltpu.SemaphoreType.DMA(())]),
        out_shape=jax.ShapeDtypeStruct((r, d), BF16),
        compiler_params=_params(("arbitrary",)),
    )(idx, src)


def _moe_up_body(be_ref, bv_ref, x_ref, w1_ref, w3_ref, o_ref):
    @pl.when(bv_ref[pl.program_id(1)] > 0)
    def _():
        _glu_body(x_ref, w1_ref, w3_ref, o_ref)

    @pl.when(bv_ref[pl.program_id(1)] == 0)
    def _():
        o_ref[...] = jnp.zeros_like(o_ref)


def moe_up(xg, w1, w3, mi, block_exp, block_valid, bm):
    r, k = xg.shape
    n = w1.shape[3]
    tn = 512
    wspec = pl.BlockSpec((None, None, k, tn), lambda j, i, be, bv: (mi, be[i], 0, j))
    return pl.pallas_call(
        _moe_up_body, name="moe_up",
        grid_spec=pltpu.PrefetchScalarGridSpec(
            num_scalar_prefetch=2, grid=(n // tn, r // bm),
            in_specs=[pl.BlockSpec((bm, k), lambda j, i, be, bv: (i, 0)), wspec, wspec],
            out_specs=pl.BlockSpec((bm, tn), lambda j, i, be, bv: (i, j))),
        out_shape=jax.ShapeDtypeStruct((r, n), BF16),
        compiler_params=_params(("arbitrary", "arbitrary"), V7X_VMEM_LIMIT),
    )(block_exp, block_valid, xg, w1, w3)


def _moe_down_body(be_ref, bv_ref, h_ref, w_ref, o_ref):
    @pl.when(bv_ref[pl.program_id(1)] > 0)
    def _():
        _mm_body(h_ref, w_ref, o_ref)

    @pl.when(bv_ref[pl.program_id(1)] == 0)
    def _():
        o_ref[...] = jnp.zeros_like(o_ref)


def moe_down(hg, w2, mi, block_exp, block_valid, bm):
    r, k = hg.shape
    n = w2.shape[3]
    tn = 512
    return pl.pallas_call(
        _moe_down_body, name="moe_down",
        grid_spec=pltpu.PrefetchScalarGridSpec(
            num_scalar_prefetch=2, grid=(n // tn, r // bm),
            in_specs=[pl.BlockSpec((bm, k), lambda j, i, be, bv: (i, 0)),
                      pl.BlockSpec((None, None, k, tn), lambda j, i, be, bv: (mi, be[i], 0, j))],
            out_specs=pl.BlockSpec((bm, tn), lambda j, i, be, bv: (i, j))),
        out_shape=jax.ShapeDtypeStruct((r, n), F32),
        compiler_params=_params(("arbitrary", "arbitrary"), V7X_VMEM_LIMIT),
    )(block_exp, block_valid, hg, w2)


def _combine_body(d0_ref, d1_ref, y_ref, g0_ref, g1_ref, r_ref, g_ref, b_ref, o_ref, ob_ref, buf0, buf1, sems, *, tm):
    base = pl.program_id(0) * tm

    def row_copy(r, src_row, buf, k):
        return pltpu.make_async_copy(y_ref.at[pl.ds(src_row, 1)], buf.at[pl.ds(r, 1)], sems.at[k])

    def start(r, c):
        row_copy(r, d0_ref[base + r], buf0, 0).start()
        row_copy(r, d1_ref[base + r], buf1, 1).start()
        return c

    def wait(r, c):
        row_copy(r, 0, buf0, 0).wait()
        row_copy(r, 0, buf1, 1).wait()
        return c

    lax.fori_loop(0, tm, start, 0)
    lax.fori_loop(0, tm, wait, 0)
    f = g0_ref[...] * buf0[...] + g1_ref[...] * buf1[...]
    y = _ln_rows(ALPHA * r_ref[...] + f, g_ref[...], b_ref[...])
    o_ref[...] = y
    ob_ref[...] = y.astype(BF16)


def moe_combine_ln(y_rows, d0, d1, g0, g1, resid, g, b):
    m, n = resid.shape
    tm = _tile(m, 256, 16)
    row = pl.BlockSpec((tm, n), lambda i, a, c: (i, 0))
    col = pl.BlockSpec((tm, 1), lambda i, a, c: (i, 0))
    vec = pl.BlockSpec((1, n), lambda i, a, c: (0, 0))
    return pl.pallas_call(
        functools.partial(_combine_body, tm=tm), name="moe_combine",
        grid_spec=pltpu.PrefetchScalarGridSpec(
            num_scalar_prefetch=2, grid=(m // tm,),
            in_specs=[pl.BlockSpec(memory_space=pl.ANY), col, col, row, vec, vec],
            out_specs=[row, row],
            scratch_shapes=[pltpu.VMEM((tm, n), F32), pltpu.VMEM((tm, n), F32), pltpu.SemaphoreType.DMA((2,))]),
        out_shape=[jax.ShapeDtypeStruct((m, n), F32), jax.ShapeDtypeStruct((m, n), BF16)],
        compiler_params=_params(("arbitrary",), V7X_VMEM_LIMIT),
    )(d0, d1, y_rows, g0, g1, resid, g.reshape(1, n), b.reshape(1, n))


def moe_layer(x, xb, router_w, w1, w3, w2, mi, ln_g, ln_b):
    m = x.shape[0]
    bm = MOE_BLOCK_ROWS
    logits = router_logits(x, router_w, mi)[:, :N_EXPERTS]
    eidx = lax.broadcasted_iota(jnp.int32, logits.shape, 1)
    v1 = jnp.max(logits, axis=-1, keepdims=True)
    i1 = jnp.min(jnp.where(logits == v1, eidx, N_EXPERTS), axis=-1, keepdims=True)
    rest = jnp.where(eidx == i1, -jnp.inf, logits)
    v2 = jnp.max(rest, axis=-1, keepdims=True)
    i2 = jnp.min(jnp.where(rest == v2, eidx, N_EXPERTS), axis=-1, keepdims=True)
    top_i = jnp.concatenate([i1, i2], axis=1)
    gates = jax.nn.softmax(jnp.concatenate([v1, v2], axis=1), axis=-1)
    e_flat = top_i.reshape(-1)
    n_assign = m * TOP_K
    onehot = (e_flat[:, None] == jnp.arange(N_EXPERTS, dtype=e_flat.dtype)[None, :]).astype(jnp.int32)
    rank = jnp.take_along_axis(jnp.cumsum(onehot, axis=0) - onehot, e_flat[:, None], axis=1)[:, 0]
    counts = jnp.sum(onehot, axis=0)
    padded = (counts + bm - 1) // bm * bm
    pad_end = jnp.cumsum(padded)
    pad_start = pad_end - padded
    dest = (pad_start[e_flat] + rank).astype(jnp.int32)
    n_blocks = -(-(n_assign + N_EXPERTS * (bm - 1)) // bm)
    rows = n_blocks * bm
    row_tok = jnp.zeros((rows,), jnp.int32).at[dest].set(jnp.arange(n_assign, dtype=jnp.int32) // TOP_K)
    blk_start = jnp.arange(n_blocks, dtype=jnp.int32) * bm
    block_exp = jnp.minimum(jnp.sum((pad_end[None, :] <= blk_start[:, None]).astype(jnp.int32), axis=1),
                            N_EXPERTS - 1)
    block_valid = (blk_start < pad_end[-1]).astype(jnp.int32)
    xg = gather_rows_bf16(x, row_tok)
    hg = moe_up(xg, w1, w3, mi, block_exp, block_valid, bm)
    yg = moe_down(hg, w2, mi, block_exp, block_valid, bm)
    dest2 = dest.reshape(m, TOP_K)
    return moe_combine_ln(yg, dest2[:, 0], dest2[:, 1], gates[:, 0:1], gates[:, 1:2], x, ln_g, ln_b)


def _rope_tables(pos):
    half = ROT_DIM // 2
    inv = ROPE_THETA ** (-2.0 * jnp.arange(half, dtype=F32) / ROT_DIM)
    ang = pos.astype(F32)[:, None] * inv[None, :]
    cos, sin = jnp.cos(ang), jnp.sin(ang)
    t = pos.shape[0]
    z8 = jnp.zeros((t, half), F32)
    z48 = jnp.zeros((t, A_DH - ROT_DIM), F32)
    c64 = jnp.concatenate([cos, cos, jnp.ones((t, A_DH - ROT_DIM), F32)], axis=1)
    s_up = jnp.concatenate([z8, sin, z48], axis=1)
    s_dn = jnp.concatenate([-sin, z8, z48], axis=1)
    return tuple(jnp.tile(a, (1, LANES // A_DH)) for a in (c64, s_up, s_dn))


def _rope_body(q_ref, k_ref, c_ref, su_ref, sd_ref, qo_ref, ko_ref):
    c, su, sd = c_ref[...], su_ref[...], sd_ref[...]
    half = ROT_DIM // 2
    for g in range(GROUP_WIDTH // LANES):
        sl = slice(g * LANES, (g + 1) * LANES)
        for src, dst, scale in ((q_ref, qo_ref, 1.0 / math.sqrt(A_DH)), (k_ref, ko_ref, 1.0)):
            x = src[:, sl]
            y = x * c + pltpu.roll(x, half, 1) * su + pltpu.roll(x, LANES - half, 1) * sd
            dst[:, sl] = (y * scale).astype(dst.dtype)


def rope_qk(proj, tables, b, t):
    rows = b * t
    tq = _tile(t, 1024, SEQ_PAD)
    nq = t // tq
    gw = GROUP_WIDTH
    tab = pl.BlockSpec((tq, LANES), lambda i: (i % nq, 0))
    return pl.pallas_call(
        _rope_body, name="rope", grid=(rows // tq,),
        in_specs=[pl.BlockSpec((tq, gw), lambda i: (i, COL_Q // gw)),
                  pl.BlockSpec((tq, gw), lambda i: (i, COL_K // gw)), tab, tab, tab],
        out_specs=[pl.BlockSpec((tq, gw), lambda i: (i, 0)), pl.BlockSpec((tq, gw), lambda i: (i, 0))],
        out_shape=[jax.ShapeDtypeStruct((rows, gw), BF16), jax.ShapeDtypeStruct((rows, gw), F32)],
        compiler_params=_params(("parallel",)),
    )(proj, proj, *tables)


def _stack_maps(q):
    lane = lax.broadcasted_iota(jnp.int32, q.shape, 1)
    zero = jnp.zeros_like(q)
    return jnp.concatenate([jnp.where(lane < A_DH, q, zero), jnp.where(lane >= A_DH, q, zero)], axis=0)


def _diff_finalize(lam_ref, g_ref, l, acc, tq, lam_init):
    lv = lam_ref[...]
    lam = (jnp.exp(jnp.sum(lv[0:1] * lv[1:2], axis=-1, keepdims=True))
           - jnp.exp(jnp.sum(lv[2:3] * lv[3:4], axis=-1, keepdims=True)) + lam_init)
    o = acc[0:tq] / l[0:tq] - lam * (acc[tq:2 * tq] / l[tq:2 * tq])
    o = o * lax.rsqrt(jnp.mean(o * o, axis=-1, keepdims=True) + RMS_EPS) * g_ref[...] * (1.0 - lam_init)
    return o


def _flash_body(lam_ref, g_ref, q_ref, k_ref, v_ref, o_ref, qq_ref, m_ref, acc_ref, *, tq, nk, lam_init):
    qi = pl.program_id(1)
    ki = pl.program_id(2)
    dv = A_DV

    @pl.when(ki == 0)
    def _():
        for h in range(A_HEADS):
            qq_ref[h] = _stack_maps(q_ref[:, h * dv:(h + 1) * dv])
        m_ref[...] = jnp.full_like(m_ref, NEG_BIG)
        acc_ref[...] = jnp.zeros_like(acc_ref)

    def step(diagonal):
        ones = jnp.ones((tq, dv), BF16)
        for h in range(A_HEADS):
            sl = slice(h * dv, (h + 1) * dv)
            s = _dot_nt(qq_ref[h], k_ref[:, sl].astype(BF16))
            if diagonal:
                r = lax.broadcasted_iota(jnp.int32, s.shape, 0)
                r = jnp.where(r >= tq, r - tq, r)
                c = lax.broadcasted_iota(jnp.int32, s.shape, 1)
                s = jnp.where(c <= r, s, NEG_BIG)
            m_prev = m_ref[h]
            m_new = jnp.maximum(m_prev, jnp.max(s, axis=-1, keepdims=True))
            a = jnp.exp(m_prev - m_new)
            p = jnp.exp(s - m_new).astype(BF16)
            v1 = jnp.concatenate([v_ref[:, sl].astype(BF16), ones], axis=1)
            acc_ref[h] = a * acc_ref[h] + _dot(p, v1)
            m_ref[h] = m_new

    @pl.when(ki < qi)
    def _():
        step(False)

    @pl.when(ki == qi)
    def _():
        step(True)

    @pl.when(ki == nk - 1)
    def _():
        for h in range(A_HEADS):
            acc = acc_ref[h]
            o = _diff_finalize(lam_ref, g_ref, acc[:, dv:2 * dv], acc[:, 0:dv], tq, lam_init)
            o_ref[:, h * dv:(h + 1) * dv] = o.astype(o_ref.dtype)


def flash_diff_attention(qs, kr, proj, lam_vec, norm_g, b, t, lam_init):
    tq = _tile(t, 512, SEQ_PAD)
    nq = t // tq
    gw = GROUP_WIDTH
    kv_map = lambda col: (lambda bi, qi, ki: (bi * nq + jnp.minimum(ki, qi), col))
    qmap = lambda bi, qi, ki: (bi * nq + qi, 0)
    return pl.pallas_call(
        functools.partial(_flash_body, tq=tq, nk=nq, lam_init=lam_init), name="flash_diff_attn",
        grid=(b, nq, nq),
        in_specs=[pl.BlockSpec((4, A_DH), lambda bi, qi, ki: (0, 0)),
                  pl.BlockSpec((1, A_DV), lambda bi, qi, ki: (0, 0)),
                  pl.BlockSpec((tq, gw), qmap), pl.BlockSpec((tq, gw), kv_map(0)),
                  pl.BlockSpec((tq, gw), kv_map(COL_V // gw))],
        out_specs=pl.BlockSpec((tq, gw), qmap),
        out_shape=jax.ShapeDtypeStruct((b * t, gw), BF16),
        scratch_shapes=[pltpu.VMEM((A_HEADS, 2 * tq, A_DV), BF16), pltpu.VMEM((A_HEADS, 2 * tq, 1), F32),
                        pltpu.VMEM((A_HEADS, 2 * tq, 2 * A_DV), F32)],
        compiler_params=_params(("parallel", "parallel", "arbitrary"), V7X_VMEM_LIMIT),
    )(lam_vec, norm_g.reshape(1, A_DV), qs, kr, proj)


SAMPLE_Q_ROWS = 16


def _paged_body(pt_ref, lam_ref, g_ref, q_ref, *refs, n_steps, pages_per_step, lam_init):
    g_pages = pages_per_step
    kc_refs, vc_refs = refs[0:g_pages], refs[g_pages:2 * g_pages]
    kn_ref, vn_ref, o_ref, qq_ref, m_ref, l_ref, acc_ref, bias_ref = refs[2 * g_pages:]
    p = pl.program_id(1)
    tq = SAMPLE_Q_ROWS
    hr = 2 * tq
    dv = A_DV

    @pl.when(p == 0)
    def _():
        for h in range(A_HEADS):
            qq_ref[h * hr:(h + 1) * hr] = _stack_maps(q_ref[:, h * dv:(h + 1) * dv])
        m_ref[...] = jnp.full_like(m_ref, NEG_BIG)
        l_ref[...] = jnp.zeros_like(l_ref)
        acc_ref[...] = jnp.zeros_like(acc_ref)
        r = lax.broadcasted_iota(jnp.int32, bias_ref.shape, 0) // hr
        c = lax.broadcasted_iota(jnp.int32, bias_ref.shape, 1) % A_HEADS
        bias_ref[...] = jnp.where(r == c, 0.0, NEG_BIG)

    def online_update(s, pv_of):
        m_prev = m_ref[...]
        m_new = jnp.maximum(m_prev, jnp.max(s, axis=-1, keepdims=True))
        a = jnp.exp(m_prev - m_new)
        pr = jnp.exp(s - m_new)
        l_ref[...] = a * l_ref[...] + jnp.sum(pr, axis=-1, keepdims=True)
        acc_ref[...] = a * acc_ref[...] + pv_of(pr.astype(BF16))
        m_ref[...] = m_new

    @pl.when(p < n_steps)
    def _():
        k2 = jnp.concatenate([kc_refs[g][...].astype(BF16) for g in range(g_pages)], axis=0)
        v2 = jnp.concatenate([vc_refs[g][...].astype(BF16) for g in range(g_pages)], axis=0)
        online_update(_dot_nt(qq_ref[...], k2) + bias_ref[...], lambda pb: _dot(pb, v2))

    @pl.when(p == n_steps)
    def _():
        s = jnp.concatenate([_dot_nt(qq_ref[h * hr:(h + 1) * hr], kn_ref[:, h * dv:(h + 1) * dv].astype(BF16))
                             for h in range(A_HEADS)], axis=0)
        r = lax.broadcasted_iota(jnp.int32, s.shape, 0) % tq
        c = lax.broadcasted_iota(jnp.int32, s.shape, 1)
        s = jnp.where(c <= r, s, NEG_BIG)
        online_update(s, lambda pb: jnp.concatenate(
            [_dot(pb[h * hr:(h + 1) * hr], vn_ref[:, h * dv:(h + 1) * dv].astype(BF16)) for h in range(A_HEADS)],
            axis=0))
        for h in range(A_HEADS):
            o = _diff_finalize(lam_ref, g_ref, l_ref[h * hr:(h + 1) * hr], acc_ref[h * hr:(h + 1) * hr], tq, lam_init)
            o_ref[:, h * dv:(h + 1) * dv] = o.astype(o_ref.dtype)


def paged_diff_attention(qs, kr, proj, cache_k, cache_v, page_table, layer, lam_vec, norm_g, lam_init):
    bs, n_pages = page_table.shape
    gw = GROUP_WIDTH
    tq = SAMPLE_Q_ROWS
    g_pages = 4 if n_pages % 4 == 0 else (2 if n_pages % 2 == 0 else 1)
    n_steps = n_pages // g_pages
    rows = A_HEADS * 2 * tq

    prows = PAGE_SIZE * A_HEADS
    ck = cache_k.reshape(cache_k.shape[0], DEPTH, prows, A_DV)
    cv = cache_v.reshape(cache_v.shape[0], DEPTH, prows, A_DV)

    def page(g):
        return lambda bi, p, pt: (pt[bi * n_pages + jnp.minimum(p, n_steps - 1) * g_pages + g], layer, 0, 0)

    const = lambda bi, p, pt: (0, 0)
    cache_specs = [pl.BlockSpec((None, None, prows, A_DV), page(g)) for g in range(g_pages)]
    return pl.pallas_call(
        functools.partial(_paged_body, n_steps=n_steps, pages_per_step=g_pages, lam_init=lam_init),
        name="paged_diff_attn",
        grid_spec=pltpu.PrefetchScalarGridSpec(
            num_scalar_prefetch=1, grid=(bs, n_steps + 1),
            in_specs=[pl.BlockSpec((4, A_DH), const), pl.BlockSpec((1, A_DV), const),
                      pl.BlockSpec((tq, gw), lambda bi, p, pt: (bi * (SEQ_PAD // tq), 0))]
            + cache_specs + cache_specs
            + [pl.BlockSpec((SEQ_PAD, gw), lambda bi, p, pt: (bi, 0)),
               pl.BlockSpec((SEQ_PAD, gw), lambda bi, p, pt: (bi, COL_V // gw))],
            out_specs=pl.BlockSpec((tq, gw), lambda bi, p, pt: (bi, 0)),
            scratch_shapes=[pltpu.VMEM((rows, A_DV), BF16), pltpu.VMEM((rows, 1), F32),
                            pltpu.VMEM((rows, 1), F32), pltpu.VMEM((rows, A_DV), F32),
                            pltpu.VMEM((rows, g_pages * prows), F32)]),
        out_shape=jax.ShapeDtypeStruct((bs * tq, gw), BF16),
        compiler_params=_params(("parallel", "arbitrary"), V7X_VMEM_LIMIT),
    )(page_table.reshape(-1), lam_vec, norm_g.reshape(1, A_DV), qs, *([ck] * g_pages), *([cv] * g_pages), kr, proj)


def _ssd_body(z_ref, xbc_ref, dtr_ref, cw_ref, cb_ref, dtb_ref, alog_ref, dsk_ref, ng_ref, conv0_ref, ssm0_ref,
              o_ref, ssml_ref, win_ref, st_ref, *, nc, t_real):
    c = pl.program_id(1)
    q = SEQ_PAD
    gw = GROUP_WIDTH

    @pl.when(c == 0)
    def _():
        win_ref[0:8, :] = conv0_ref[...]
        st_ref[...] = ssm0_ref[...]

    win_ref[8:8 + q, :] = xbc_ref[...]
    conv = cb_ref[...]
    for j in range(B_CONV):
        conv = conv + win_ref[pl.ds(8 - (B_CONV - 1) + j, q), :] * cw_ref[j:j + 1, :]
    win_ref[0:8, :] = win_ref[q:q + 8, :]
    act = conv * _sigmoid(conv)
    xs = act[:, 0:gw]
    bmat = act[:, gw:gw + 2 * B_STATE]
    cmat = act[:, gw + 2 * B_STATE:gw + 4 * B_STATE]

    row = lax.broadcasted_iota(jnp.int32, (q, q), 0)
    lane = lax.broadcasted_iota(jnp.int32, (q, q), 1)
    dt = _softplus(dtr_ref[...] + dtb_ref[...])
    dt = jnp.where((c * q + row < t_real) & (lane < B_HEADS), dt, 0.0)
    dta = dt * (-jnp.exp(alog_ref[...]))
    causal = row >= lane
    acum = jnp.dot(causal.astype(F32), dta, precision=HIGHEST, preferred_element_type=F32)
    acum_t = acum.T
    dt_t = dt.T
    first_half_l = lane < B_HEADDIM
    first_half_r = row < B_HEADDIM

    y_pairs = []
    for g in range(2):
        cg = cmat[:, g * B_STATE:(g + 1) * B_STATE].astype(BF16)
        bg = bmat[:, g * B_STATE:(g + 1) * B_STATE].astype(BF16)
        cb = _dot_nt(cg, bg)
        for pp in range(2):
            p = 2 * g + pp
            xs_pair = xs[:, p * LANES:(p + 1) * LANES]
            st = st_ref[p]
            y_pair = jnp.zeros((q, LANES), F32)
            for hh in range(2):
                h = 2 * p + hh
                seg = acum[:, h:h + 1] - acum_t[h:h + 1, :]
                w = cb * jnp.exp(jnp.where(causal, seg, NEG_BIG)) * dt_t[h:h + 1, :]
                xh = jnp.where(first_half_l if hh == 0 else ~first_half_l, xs_pair, 0.0)
                y_pair = y_pair + _dot(w.astype(BF16), xh.astype(BF16))
            e_in = jnp.where(first_half_l, jnp.exp(acum[:, 2 * p:2 * p + 1]), jnp.exp(acum[:, 2 * p + 1:2 * p + 2]))
            y_pair = y_pair + e_in * _dot_nt(cg, st.astype(BF16))
            y_pairs.append(y_pair)
            al0 = acum[q - 1:q, 2 * p:2 * p + 1]
            al1 = acum[q - 1:q, 2 * p + 1:2 * p + 2]
            te0 = jnp.exp(al0 - acum_t[2 * p:2 * p + 1, :]) * dt_t[2 * p:2 * p + 1, :]
            te1 = jnp.exp(al1 - acum_t[2 * p + 1:2 * p + 2, :]) * dt_t[2 * p + 1:2 * p + 2, :]
            xw = xs_pair.T * jnp.where(first_half_r, te0, te1)
            st_ref[p] = jnp.where(first_half_r, jnp.exp(al0), jnp.exp(al1)) * st + _dot(xw.astype(BF16), bg)

    y = jnp.concatenate(y_pairs, axis=1) + xs * dsk_ref[...]
    z = z_ref[...]
    gt = y * (z * _sigmoid(z))
    half = gw // 2
    outs = []
    for g in range(2):
        gg = gt[:, g * half:(g + 1) * half]
        outs.append(gg * lax.rsqrt(jnp.mean(gg * gg, axis=-1, keepdims=True) + RMS_EPS))
    o_ref[...] = (jnp.concatenate(outs, axis=1) * ng_ref[...]).astype(o_ref.dtype)

    @pl.when(c == nc - 1)
    def _():
        ssml_ref[...] = st_ref[...]


def ssd_mixer(proj, dtr, conv0, ssm0, conv_w, conv_b, dt_bias, a_log, d_skip, norm_g, b, t, t_real):
    q = SEQ_PAD
    nc = t // q
    gw = GROUP_WIDTH
    rowblk = lambda width, col: pl.BlockSpec((q, width), lambda bi, c: (bi * nc + c, col))
    vec = lambda width: pl.BlockSpec((1, width), lambda bi, c: (0, 0))
    pad8 = lambda a: jnp.zeros((LANES,), F32).at[:B_HEADS].set(a).reshape(1, LANES)
    conv0p = jnp.pad(conv0, ((0, 0), (8 - (B_CONV - 1), 0), (0, 0)))
    ssm0p = ssm0.reshape(b, B_HEADS // 2, 2 * B_HEADDIM, B_STATE)
    o, ssml = pl.pallas_call(
        functools.partial(_ssd_body, nc=nc, t_real=t_real), name="ssd", grid=(b, nc),
        in_specs=[rowblk(gw, COL_Z // gw), rowblk(B_CONV_CH, COL_XBC // B_CONV_CH), rowblk(LANES, 0),
                  pl.BlockSpec((B_CONV, B_CONV_CH), lambda bi, c: (0, 0)), vec(B_CONV_CH), vec(LANES), vec(LANES),
                  vec(gw), vec(gw),
                  pl.BlockSpec((None, 8, B_CONV_CH), lambda bi, c: (bi, 0, 0)),
                  pl.BlockSpec((None, B_HEADS // 2, 2 * B_HEADDIM, B_STATE), lambda bi, c: (bi, 0, 0, 0))],
        out_specs=[pl.BlockSpec((q, gw), lambda bi, c: (bi * nc + c, 0)),
                   pl.BlockSpec((None, B_HEADS // 2, 2 * B_HEADDIM, B_STATE), lambda bi, c: (bi, 0, 0, 0))],
        out_shape=[jax.ShapeDtypeStruct((b * t, gw), BF16),
                   jax.ShapeDtypeStruct((b, B_HEADS // 2, 2 * B_HEADDIM, B_STATE), F32)],
        scratch_shapes=[pltpu.VMEM((8 + q, B_CONV_CH), F32),
                        pltpu.VMEM((B_HEADS // 2, 2 * B_HEADDIM, B_STATE), F32)],
        compiler_params=_params(("parallel", "arbitrary")),
    )(proj, proj, dtr, conv_w, conv_b.reshape(1, -1), pad8(dt_bias), pad8(a_log),
      jnp.repeat(d_skip, B_HEADDIM).reshape(1, gw), norm_g.reshape(1, gw), conv0p, ssm0p)
    return o, ssml.reshape(b, B_HEADS, B_HEADDIM, B_STATE)


def _rwkv_prep_body(r_ref, k_ref, v_ref, u_ref, sh_ref, vf_ref, mu_ref, w0_ref, w1_ref, w2_ref, a0_ref, a1_ref, a2_ref,
                    g1_ref, g2_ref, kk_ref, ka_ref, rk_ref, vmu_ref, v0_ref, v1_ref, v2_ref, seg_ref,
                    nkk_o, dr_o, d_o, b_o, km_o, v_o, br_o, kr_o, g_o, bon_o, carry_ref, *, first_layer):
    c = pl.program_id(1)
    gw = GROUP_WIDTH

    @pl.when(c == 0)
    def _():
        carry_ref[...] = sh_ref[...]

    row0 = lax.broadcasted_iota(jnp.int32, r_ref.shape, 0) == 0

    def delta(ref, i):
        x = ref[...]
        prev = jnp.where(row0, carry_ref[:, i * gw:(i + 1) * gw], pltpu.roll(x, 1, 0))
        carry_ref[:, i * gw:(i + 1) * gw] = x[x.shape[0] - 1:x.shape[0]]
        return x, prev - x

    r_in, d_r = delta(r_ref, 0)
    k_in, d_k = delta(k_ref, 1)
    v_in, d_v = delta(v_ref, 2)
    u_in, d_u = delta(u_ref, 3)
    mu = mu_ref[...]
    r = r_in + d_r * mu[0:1]
    k = k_in + d_k * mu[1:2]
    v = v_in + d_v * mu[2:3]
    xw = u_in + d_u * mu[3:4]
    xa = u_in + d_u * mu[4:5]
    xg = u_in + d_u * mu[5:6]

    def lora(x, a_ref, b_ref, act):
        return _dot(act(_dot(x.astype(BF16), a_ref[...])).astype(BF16), b_ref[...])

    w = -_softplus(-(w0_ref[...] + lora(xw, w1_ref, w2_ref, jnp.tanh))) - 0.5
    a = _sigmoid(a0_ref[...] + lora(xa, a1_ref, a2_ref, lambda t: t))
    g = lora(xg, g1_ref, g2_ref, _sigmoid)
    if not first_layer:
        xv = u_in + d_u * vmu_ref[...]
        v = v + (vf_ref[...] - v) * _sigmoid(v0_ref[...] + lora(xv, v1_ref, v2_ref, lambda t: t))
    seg = seg_ref[...]
    kk = k * kk_ref[...]
    kk = kk / jnp.maximum(jnp.sqrt(_seg_dot(kk * kk, seg)), 1e-12)
    km = k * (1.0 + (a - 1.0) * ka_ref[...])
    d = jnp.exp(-jnp.exp(w))
    bvec = kk * a
    nkk_o[...] = -kk
    dr_o[...] = d * r
    d_o[...] = d
    b_o[...] = bvec
    km_o[...] = km
    v_o[...] = v
    br_o[...] = _seg_dot(bvec * r, seg)
    kr_o[...] = _seg_dot(km * r, seg)
    g_o[...] = g
    bon_o[...] = _seg_dot(r * km * rk_ref[...], seg) * v


def _pad_cols(w, n):
    return jnp.pad(w, ((0, 0), (0, n - w.shape[1])))


def _pad_rows(w, n):
    return jnp.pad(w, ((0, n - w.shape[0]), (0, 0)))


def _head_seg_matrix(scale):
    h = jnp.arange(GROUP_WIDTH) // C_HEAD
    return ((h[:, None] == h[None, :]).astype(F32) * scale).astype(BF16)


def rwkv_prep(proj, shift0, v_first, lp, b, t, first_layer):
    q = SEQ_PAD
    nc = t // q
    gw = GROUP_WIDTH
    rowblk = lambda col: pl.BlockSpec((q, gw), lambda bi, c: (bi * nc + c, col))
    full = lambda a: pl.BlockSpec(a.shape, lambda bi, c: (0,) * a.ndim)
    row1 = lambda a: a.reshape(1, gw)
    lo = lambda a, bmat: (_pad_cols(a, LANES).astype(BF16), _pad_rows(bmat, LANES).astype(BF16))
    w1, w2 = lo(lp['w1'], lp['w2'])
    a1, a2 = lo(lp['a1'], lp['a2'])
    g1, g2 = lo(lp['g1'], lp['g2'])
    v1, v2 = lo(lp['v1'], lp['v2'])
    params = [lp['mu'], row1(lp['w0']), w1, w2, row1(lp['a0']), a1, a2, g1, g2, row1(lp['kk']), row1(lp['ka']),
              row1(lp['rk']), row1(lp['vmu']), row1(lp['v0']), v1, v2, _head_seg_matrix(1.0)]
    outs = pl.pallas_call(
        functools.partial(_rwkv_prep_body, first_layer=first_layer), name="rwkv_prep", grid=(b, nc),
        in_specs=[rowblk(COL_RKVU // gw), rowblk(COL_RKVU // gw + 1), rowblk(COL_RKVU // gw + 2),
                  rowblk(COL_RKVU // gw + 3), pl.BlockSpec((None, 1, 4 * gw), lambda bi, c: (bi, 0, 0)),
                  rowblk(0)] + [full(a) for a in params],
        out_specs=[rowblk(0)] * 10,
        out_shape=[jax.ShapeDtypeStruct((b * t, gw), F32)] * 10,
        scratch_shapes=[pltpu.VMEM((1, 4 * gw), F32)],
        compiler_params=_params(("parallel", "arbitrary")),
    )(proj, proj, proj, proj, shift0.reshape(b, 1, 4 * gw), v_first, *params)
    return outs


RWKV_MXU_BROADCASTS = 4


def _rwkv_scan_body(nkk_ref, dr_ref, d_ref, b_ref, km_ref, v_ref, br_ref, kr_ref, s0_ref, o_ref, sl_ref, st_ref,
                    cb_ref, *, nc, tc, t_real):
    c = pl.program_id(1)
    n = C_HEAD
    npairs = C_HEADS // 2
    col_refs = (nkk_ref, dr_ref, d_ref, b_ref, km_ref)

    @pl.when(c == 0)
    def _():
        st_ref[...] = s0_ref[...]

    @pl.when(c * tc >= t_real)
    def _():
        o_ref[...] = jnp.zeros_like(o_ref)

    @pl.when(c * tc < t_real)
    def _():
        key = lax.broadcasted_iota(jnp.int32, (tc, n, LANES), 1)
        lane = lax.broadcasted_iota(jnp.int32, (tc, n, LANES), 2)
        diag = (lane == key) | (lane == key + n)
        r_h = lax.broadcasted_iota(jnp.int32, (2 * LANES, LANES), 0) % LANES // n
        c_h = lax.broadcasted_iota(jnp.int32, (2 * LANES, LANES), 1) // n
        spread = (r_h == c_h).astype(BF16)
        first = lax.broadcasted_iota(jnp.int32, (n, LANES), 1) < n
        zpad = jnp.zeros((LANES - tc, LANES), F32)

        def stage_columns(p):
            for x, ref in enumerate(col_refs[:RWKV_MXU_BROADCASTS]):
                v0 = ref[:, p * LANES:(p + 1) * LANES]
                hi = v0.astype(BF16)
                lo = (v0 - hi.astype(F32)).astype(BF16)
                pieces = [jnp.where(diag, jnp.broadcast_to(pc.astype(F32)[:, None, :], (tc, n, LANES)), 0.0)
                          .astype(BF16).reshape(tc * n, LANES) for pc in (hi, lo)]
                cb = _dot(jnp.concatenate(pieces, axis=1), spread)
                cb_ref[x, p] = cb.reshape(tc, n, LANES)
            for x in range(RWKV_MXU_BROADCASTS, len(col_refs)):
                xt = jnp.concatenate([col_refs[x][:, p * LANES:(p + 1) * LANES], zpad], axis=0).T
                for tt in range(tc):
                    col = xt[:, tt:tt + 1]
                    cb_ref[x, p, tt] = jnp.where(first, jnp.broadcast_to(col[0:n], (n, LANES)),
                                                 jnp.broadcast_to(col[n:2 * n], (n, LANES)))

        def recur(p):
            st = st_ref[p]
            sl = slice(p * LANES, (p + 1) * LANES)
            for tt in range(tc):
                sa = jnp.sum(st * cb_ref[0, p, tt], axis=0, keepdims=True)
                u = jnp.sum(st * cb_ref[1, p, tt], axis=0, keepdims=True)
                vrow = v_ref[tt:tt + 1, sl]
                o_ref[tt:tt + 1, sl] = u + sa * br_ref[tt:tt + 1, sl] + vrow * kr_ref[tt:tt + 1, sl]
                st = st * cb_ref[2, p, tt] + cb_ref[3, p, tt] * sa + cb_ref[4, p, tt] * vrow
            st_ref[p] = st

        stage_columns(0)
        for p in range(npairs):
            if p + 1 < npairs:
                stage_columns(p + 1)
            recur(p)

    @pl.when(c == nc - 1)
    def _():
        sl_ref[...] = st_ref[...]


def rwkv_scan(nkk, dr, d, bv, km, v, br, kr, s0, b, t, t_real):
    tc = 16 if t_real % 16 == 0 else 8
    assert t_real % tc == 0 and t % tc == 0
    nc = t // tc
    gw = GROUP_WIDTH
    n = C_HEAD
    rowblk = pl.BlockSpec((tc, gw), lambda bi, c: (bi * nc + c, 0))
    sblk = pl.BlockSpec((None, C_HEADS // 2, n, 2 * n), lambda bi, c: (bi, 0, 0, 0))
    s0t = s0.reshape(b, C_HEADS // 2, 2, n, n).transpose(0, 1, 4, 2, 3).reshape(b, C_HEADS // 2, n, 2 * n)
    o, sl = pl.pallas_call(
        functools.partial(_rwkv_scan_body, nc=nc, tc=tc, t_real=t_real), name="rwkv_scan", grid=(b, nc),
        in_specs=[rowblk] * 8 + [sblk],
        out_specs=[rowblk, sblk],
        out_shape=[jax.ShapeDtypeStruct((b * t, gw), F32), jax.ShapeDtypeStruct((b, C_HEADS // 2, n, 2 * n), F32)],
        scratch_shapes=[pltpu.VMEM((C_HEADS // 2, n, 2 * n), F32),
                        pltpu.VMEM((5, C_HEADS // 2, tc, n, 2 * n), F32)],
        compiler_params=_params(("parallel", "arbitrary"), V7X_VMEM_LIMIT),
    )(nkk, dr, d, bv, km, v, br, kr, s0t)
    s_last = sl.reshape(b, C_HEADS // 2, n, 2, n).transpose(0, 1, 3, 4, 2).reshape(b, C_HEADS, n, n)
    return o, s_last


def _rwkv_post_body(o_ref, g_ref, bon_ref, lg_ref, lb_ref, seg_ref, out_ref):
    o = o_ref[...]
    seg = seg_ref[...]
    xc = o - _seg_dot(o, seg)
    var = _seg_dot(xc * xc, seg)
    y = xc * lax.rsqrt(var + C_LN_EPS) * lg_ref[...] + lb_ref[...]
    out_ref[...] = ((y + bon_ref[...]) * g_ref[...]).astype(out_ref.dtype)


def rwkv_post(o, g, bonus, ln_g, ln_b):
    rows, gw = o.shape
    q = _tile(rows, 512, 16)
    rowblk = pl.BlockSpec((q, gw), lambda i: (i, 0))
    vec = pl.BlockSpec((1, gw), lambda i: (0, 0))
    return pl.pallas_call(
        _rwkv_post_body, name="rwkv_post", grid=(rows // q,),
        in_specs=[rowblk, rowblk, rowblk, vec, vec, pl.BlockSpec((gw, gw), lambda i: (0, 0))],
        out_specs=rowblk, out_shape=jax.ShapeDtypeStruct((rows, gw), BF16),
        compiler_params=_params(("parallel",)),
    )(o, g, bonus, ln_g.reshape(1, gw), ln_b.reshape(1, gw), _head_seg_matrix(1.0 / C_HEAD))


S5_LANES = D_GROUPS * D_STATE
S5_GB = GROUP_WIDTH // LANES


def _s5_body(u_ref, bre_ref, bim_ref, cre_ref, cim_ref, are_ref, aim_ref, pre_ref, pim_ref, dsk_ref, gw_ref, gb_ref,
             h0re_ref, h0im_ref, o_ref, hlre_ref, hlim_ref, hre_ref, him_ref, cr_ref, ci_ref, *, nc, t_real):
    c = pl.program_id(1)
    q = SEQ_PAD
    sb = S5_LANES // S5_GB

    @pl.when(c == 0)
    def _():
        cr_ref[...] = h0re_ref[...]
        ci_ref[...] = h0im_ref[...]

    u = u_ref[...]
    ub = u.astype(BF16)
    for gb in range(S5_GB):
        ug = ub[:, gb * LANES:(gb + 1) * LANES]
        hre_ref[:, gb * sb:(gb + 1) * sb] = _dot(ug, bre_ref[gb])
        him_ref[:, gb * sb:(gb + 1) * sb] = _dot(ug, bim_ref[gb])

    row8 = lax.broadcasted_iota(jnp.int32, (8, S5_LANES), 0)
    pre, pim = pre_ref[...], pim_ref[...]

    def group(i, carry):
        cr, ci = carry
        off = pl.multiple_of(i * 8, 8)
        xr = hre_ref[pl.ds(off, 8), :]
        xi = him_ref[pl.ds(off, 8), :]
        for k, sh in enumerate((1, 2, 4)):
            ar, ai = are_ref[k:k + 1, :], aim_ref[k:k + 1, :]
            sr = jnp.where(row8 >= sh, pltpu.roll(xr, sh, 0), 0.0)
            si = jnp.where(row8 >= sh, pltpu.roll(xi, sh, 0), 0.0)
            xr, xi = xr + ar * sr - ai * si, xi + ar * si + ai * sr
        xr, xi = xr + pre * cr - pim * ci, xi + pre * ci + pim * cr
        hre_ref[pl.ds(off, 8), :] = xr
        him_ref[pl.ds(off, 8), :] = xi
        return xr[7:8], xi[7:8]

    cr, ci = lax.fori_loop(0, q // 8, group, (cr_ref[...], ci_ref[...]))
    cr_ref[...] = cr
    ci_ref[...] = ci

    @pl.when(c == (t_real - 1) // q)
    def _():
        rl = (t_real - 1) % q
        hlre_ref[...] = hre_ref[rl:rl + 1, :]
        hlim_ref[...] = him_ref[rl:rl + 1, :]

    ys = []
    for gb in range(S5_GB):
        hr = hre_ref[:, gb * sb:(gb + 1) * sb].astype(BF16)
        hi = him_ref[:, gb * sb:(gb + 1) * sb].astype(BF16)
        ys.append(_dot(hr, cre_ref[gb]) - _dot(hi, cim_ref[gb]))
    y = jnp.concatenate(ys, axis=1) + dsk_ref[...] * u
    y = 0.5 * y * (1.0 + jnp.tanh(math.sqrt(2.0 / math.pi) * (y + 0.044715 * (y * y * y))))
    gate = _sigmoid(_dot(y.astype(BF16), gw_ref[...]) + gb_ref[...])
    o_ref[...] = (y * gate).astype(o_ref.dtype)


def s5_params(a_re, a_im, log_dt, b_re, b_im, c_re, c_im):
    step = jnp.exp(log_dt)[:, None]
    zr, zi = a_re * step, a_im * step

    def apow(k):
        e = jnp.exp(k * zr)
        return (e * jnp.cos(k * zi)).reshape(-1), (e * jnp.sin(k * zi)).reshape(-1)

    abr, abi = jnp.exp(zr) * jnp.cos(zi), jnp.exp(zr) * jnp.sin(zi)
    den = a_re * a_re + a_im * a_im
    fr = ((abr - 1.0) * a_re + abi * a_im) / den
    fi = (abi * a_re - (abr - 1.0) * a_im) / den
    bbr = fr[..., None] * b_re - fi[..., None] * b_im
    bbi = fr[..., None] * b_im + fi[..., None] * b_re
    gpb = D_GROUPS // S5_GB
    eye = jnp.eye(gpb, dtype=F32)

    def in_blocks(x):
        x = x.reshape(S5_GB, gpb, D_STATE, D_GROUP_CH)
        return jnp.einsum('bgnc,gh->bgchn', x, eye).reshape(S5_GB, gpb * D_GROUP_CH, gpb * D_STATE).astype(BF16)

    def out_blocks(x):
        x = x.reshape(S5_GB, gpb, D_GROUP_CH, D_STATE)
        return jnp.einsum('bgcn,gh->bgnhc', x, eye).reshape(S5_GB, gpb * D_STATE, gpb * D_GROUP_CH).astype(BF16)

    pows = [apow(float(k)) for k in (1, 2, 4)]
    p8 = [apow(float(k)) for k in range(1, 9)]
    return dict(bre=in_blocks(bbr), bim=in_blocks(bbi), cre=out_blocks(c_re), cim=out_blocks(c_im),
                are=jnp.stack([p[0] for p in pows]), aim=jnp.stack([p[1] for p in pows]),
                pre=jnp.stack([p[0] for p in p8]), pim=jnp.stack([p[1] for p in p8]))


def s5_mixer(proj, sp, d_skip, glu_w, glu_b, h0re, h0im, b, t, t_real):
    q = SEQ_PAD
    nc = t // q
    gw = GROUP_WIDTH
    full = lambda a: pl.BlockSpec(a.shape, lambda bi, c: (0,) * a.ndim)
    st = pl.BlockSpec((None, 1, S5_LANES), lambda bi, c: (bi, 0, 0))
    consts = [sp['bre'], sp['bim'], sp['cre'], sp['cim'], sp['are'], sp['aim'], sp['pre'], sp['pim'],
              d_skip.reshape(1, gw), glu_w.astype(BF16), glu_b.reshape(1, gw)]
    o, hlre, hlim = pl.pallas_call(
        functools.partial(_s5_body, nc=nc, t_real=t_real), name="s5", grid=(b, nc),
        in_specs=[pl.BlockSpec((q, gw), lambda bi, c: (bi * nc + c, COL_S5 // gw))] + [full(a) for a in consts]
        + [st, st],
        out_specs=[pl.BlockSpec((q, gw), lambda bi, c: (bi * nc + c, 0)), st, st],
        out_shape=[jax.ShapeDtypeStruct((b * t, gw), BF16), jax.ShapeDtypeStruct((b, 1, S5_LANES), F32),
                   jax.ShapeDtypeStruct((b, 1, S5_LANES), F32)],
        scratch_shapes=[pltpu.VMEM((q, S5_LANES), F32), pltpu.VMEM((q, S5_LANES), F32),
                        pltpu.VMEM((1, S5_LANES), F32), pltpu.VMEM((1, S5_LANES), F32)],
        compiler_params=_params(("parallel", "arbitrary")),
    )(proj, *consts, h0re.reshape(b, 1, S5_LANES), h0im.reshape(b, 1, S5_LANES))
    return o, hlre.reshape(b, D_GROUPS, D_STATE), hlim.reshape(b, D_GROUPS, D_STATE)


def _pad_seq(a, b, t, t_pad):
    return jnp.pad(a.reshape(b, t, -1), ((0, 0), (0, t_pad - t), (0, 0))).reshape(b * t_pad, -1)


def kernel(x_prompt, x_sample, cache_k, cache_v, state_ssm, state_conv, state_wkv, state_shift, state_s5_re, state_s5_im, page_table, meta_tokens, ln_in_g, ln_in_b, w_in, w_out, ln1_g, ln1_b, ln2_g, ln2_b, a_lambda, a_norm_g, b_conv_w, b_conv_b, b_dt_bias, b_a_log, b_d, b_norm_g, c_mu, c_w0, c_w1, c_w2, c_a0, c_a1, c_a2, c_v_mu, c_v0, c_v1, c_v2, c_g1, c_g2, c_kk, c_ka, c_rk, c_ln_g, c_ln_b, d_a_re, d_a_im, d_log_dt, d_b_re, d_b_im, d_c_re, d_c_im, d_d, d_glu_w, d_glu_b, f_w1, f_w3, f_w2, m_router, m_w1, m_w3, m_w2):
    bp, seq, d = x_prompt.shape
    bs, ts, _ = x_sample.shape
    tp = seq + N_META
    tpp = -(-tp // SEQ_PAD) * SEQ_PAD
    tsp = SEQ_PAD
    rp = bp * tpp
    past_len = page_table.shape[1] * PAGE_SIZE
    gw = GROUP_WIDTH

    meta = jnp.broadcast_to(meta_tokens[None], (bp, N_META, d))
    xp = jnp.pad(jnp.concatenate([meta, x_prompt], axis=1), ((0, 0), (0, tpp - tp), (0, 0)))
    x0 = jnp.concatenate([xp.reshape(rp, d), x_sample.reshape(bs * ts, d)], axis=0)
    x, xb = ln_in(x0, ln_in_g, ln_in_b)

    dt0 = COL_RKVU
    w_main = jnp.concatenate([w_in[:, :, :dt0], w_in[:, :, dt0 + B_HEADS:]], axis=2)
    w_dt = jnp.pad(w_in[:, :, dt0:dt0 + B_HEADS], ((0, 0), (0, 0), (0, LANES - B_HEADS)))
    router_w = jnp.pad(m_router, ((0, 0), (0, 0), (0, LANES - N_EXPERTS)))
    w_out_b = w_out.astype(BF16)
    f_w2_b = f_w2.astype(BF16)

    tabs_p = _rope_tables(jnp.arange(tpp, dtype=jnp.int32))
    tabs_s = _rope_tables(past_len + jnp.arange(tsp, dtype=jnp.int32))

    zeros = lambda *s: jnp.zeros(s, F32)
    groups = [
        dict(b=bp, t=tpp, tr=tp, conv=zeros(bp, DEPTH, B_CONV - 1, B_CONV_CH),
             ssm=zeros(bp, DEPTH, B_HEADS, B_HEADDIM, B_STATE), shift=zeros(bp, DEPTH, 4 * gw),
             wkv=zeros(bp, DEPTH, C_HEADS, C_HEAD, C_HEAD), s5re=zeros(bp, DEPTH, D_GROUPS, D_STATE),
             s5im=zeros(bp, DEPTH, D_GROUPS, D_STATE), tabs=tabs_p),
        dict(b=bs, t=tsp, tr=ts, conv=state_conv, ssm=state_ssm, shift=state_shift, wkv=state_wkv,
             s5re=state_s5_re, s5im=state_s5_im, tabs=tabs_s),
    ]
    outs = [dict(k=[], v=[], ssm=[], conv=[], wkv=[], shift=[], s5re=[], s5im=[]) for _ in groups]
    v_first = [None, None]

    for l in range(DEPTH):
        lam_init = 0.8 - 0.6 * math.exp(-0.3 * l)
        proj_all = matmul_layer(xb, w_main, l, 512, "proj_in")
        dtr_all = matmul_layer(xb, w_dt, l, LANES, "proj_dt")
        sp = s5_params(d_a_re[l], d_a_im[l], d_log_dt[l], d_b_re[l], d_b_im[l], d_c_re[l], d_c_im[l])
        lv = max(l - 1, 0)
        lp = dict(mu=c_mu[l], w0=c_w0[l], w1=c_w1[l], w2=c_w2[l], a0=c_a0[l], a1=c_a1[l], a2=c_a2[l], g1=c_g1[l],
                  g2=c_g2[l], kk=c_kk[l], ka=c_ka[l], rk=c_rk[l].reshape(-1), vmu=c_v_mu[lv], v0=c_v0[lv],
                  v1=c_v1[lv], v2=c_v2[lv])
        mixes = []
        for gi, (gr, out) in enumerate(zip(groups, outs)):
            b, t, tr = gr['b'], gr['t'], gr['tr']
            if gi == 0:
                proj, dtr = proj_all, dtr_all
            else:
                proj = _pad_seq(proj_all[rp:], b, tr, t)
                dtr = _pad_seq(dtr_all[rp:], b, tr, t)
            qs, kr = rope_qk(proj, gr['tabs'], b, t)
            if gi == 0:
                o_a = flash_diff_attention(qs, kr, proj, a_lambda[l], a_norm_g[l], b, t, lam_init)
            else:
                o_a = paged_diff_attention(qs, kr, proj, cache_k, cache_v, page_table, l, a_lambda[l], a_norm_g[l],
                                           lam_init)
                o_a = _pad_seq(o_a.reshape(b, SAMPLE_Q_ROWS, gw)[:, :tr], b, tr, t)
            o_b, ssm_new = ssd_mixer(proj, dtr, gr['conv'][:, l], gr['ssm'][:, l], b_conv_w[l], b_conv_b[l],
                                     b_dt_bias[l], b_a_log[l], b_d[l], b_norm_g[l], b, t, tr)
            vf = proj[:, :gw] if l == 0 else v_first[gi]
            nkk, dr, dd, bv, km, v, br, krr, g, bonus = rwkv_prep(proj, gr['shift'][:, l], vf, lp, b, t, l == 0)
            if l == 0:
                v_first[gi] = v
            o_scan, wkv_new = rwkv_scan(nkk, dr, dd, bv, km, v, br, krr, gr['wkv'][:, l], b, t, tr)
            o_c = rwkv_post(o_scan, g, bonus, c_ln_g[l], c_ln_b[l])
            o_d, s5re_new, s5im_new = s5_mixer(proj, sp, d_d[l], d_glu_w[l], d_glu_b[l], gr['s5re'][:, l],
                                               gr['s5im'][:, l], b, t, tr)
            mix = jnp.concatenate([o_a, o_b, o_c, o_d], axis=1)
            mixes.append(mix if gi == 0 else mix.reshape(b, t, 4 * gw)[:, :tr].reshape(b * tr, 4 * gw))
            p3 = proj[:b * t].reshape(b, t, -1)
            out['k'].append(kr.reshape(b, t, A_HEADS, A_DV)[:, :tr])
            out['v'].append(p3[:, :tr, COL_V:COL_V + gw].reshape(b, tr, A_HEADS, A_DV))
            out['ssm'].append(ssm_new)
            out['conv'].append(p3[:, tr - (B_CONV - 1):tr, COL_XBC:COL_XBC + B_CONV_CH])
            out['wkv'].append(wkv_new)
            out['shift'].append(p3[:, tr - 1, COL_RKVU:COL_RKVU + 4 * gw])
            out['s5re'].append(s5re_new)
            out['s5im'].append(s5im_new)
        mix_all = jnp.concatenate(mixes, axis=0)
        x, xb = matmul_resid_ln(mix_all, w_out_b, l, x, ln1_g[l], ln1_b[l], "proj_out")
        if l % 2 == 0:
            hb = swiglu_up(xb, f_w1, f_w3, l // 2)
            x, xb = matmul_resid_ln(hb, f_w2_b, l // 2, x, ln2_g[l], ln2_b[l], "ffn_down")
        else:
            x, xb = moe_layer(x, xb, router_w, m_w1, m_w3, m_w2, l // 2, ln2_g[l], ln2_b[l])

    y_p = x[:rp].reshape(bp, tpp, d)[:, N_META:tp]
    y_s = x[rp:].reshape(bs, ts, d)
    st = lambda gi, key: jnp.stack(outs[gi][key], axis=1)
    return (y_p, y_s, st(0, 'k'), st(0, 'v'), st(1, 'k'), st(1, 'v'), st(0, 'ssm'), st(1, 'ssm'),
            st(0, 'conv'), st(1, 'conv'), st(0, 'wkv'), st(1, 'wkv'), st(0, 'shift'), st(1, 'shift'),
            st(0, 's5re'), st(0, 's5im'), st(1, 's5re'), st(1, 's5im'))
```
